```python
import jax, jax.numpy as jnp
from jax import lax
import numpy as np

D_MODEL = 1024
BATCH = 1
SEQ = 16384
DEPTH = 1
DEC_BATCH = 4
DEC_SEQ = 4096
PAST_LEN = 128

N_Q_HEADS = 8
N_KV_HEADS = 2
HEAD_DIM = 64
ATT_WIDTH = N_Q_HEADS * HEAD_DIM
KV_WIDTH = N_KV_HEADS * HEAD_DIM
ROPE_THETA = 10000.0
ROPE_AXIS_DIM = HEAD_DIM // 2
GRID_W = 64
Q_BLOCK = 128
HG_HEADS = 8
HG_KDIM = 64
HG_VDIM = 64
HG_WIDTH = HG_HEADS * HG_VDIM
HG_CHUNK = 64
MIX_WIDTH = ATT_WIDTH + HG_WIDTH
IN_SPLIT_SIZES = (ATT_WIDTH, KV_WIDTH, KV_WIDTH, HG_WIDTH, HG_WIDTH, HG_WIDTH, HG_WIDTH, HG_WIDTH)
IN_WIDTH = ATT_WIDTH + 2 * KV_WIDTH + 5 * HG_WIDTH
N_EXPERTS = 64
N_GROUPS = 8
EXPERTS_PER_GROUP = N_EXPERTS // N_GROUPS
TOPK_GROUPS = 4
TOP_K = 8
EXPERT_HIDDEN = 256
SHARED_HIDDEN = 256
ROUTED_SCALE = 2.5
DEEPNORM_ALPHA = (2.0 * DEPTH) ** 0.25
DEEPNORM_BETA = (8.0 * DEPTH) ** -0.25
RMS_EPS = 1e-6
LN_EPS = 1e-5

kernel_name = 'hymba_gqa_hgrn2_moe_deepnorm_encoder'


def rms_norm(x, w):
    xf = x.astype(jnp.float32)
    y = xf * lax.rsqrt(jnp.mean(xf * xf, axis=-1, keepdims=True) + RMS_EPS)
    return (y * w.astype(jnp.float32)).astype(x.dtype)


def layer_norm(x, w, b):
    xf = x.astype(jnp.float32)
    mu = jnp.mean(xf, axis=-1, keepdims=True)
    xc = xf - mu
    var = jnp.mean(xc * xc, axis=-1, keepdims=True)
    y = xc * lax.rsqrt(var + LN_EPS) * w.astype(jnp.float32) + b.astype(jnp.float32)
    return y.astype(x.dtype)


def axial_rope_tables(seq):
    rows = seq // GRID_W
    row_ids = jnp.repeat(jnp.arange(rows, dtype=jnp.float32), GRID_W)
    col_ids = jnp.tile(jnp.arange(GRID_W, dtype=jnp.float32), rows)
    inv_freq = ROPE_THETA ** (-jnp.arange(0, ROPE_AXIS_DIM, 2, dtype=jnp.float32) / ROPE_AXIS_DIM)
    ang_r = row_ids[:, None] * inv_freq
    ang_c = col_ids[:, None] * inv_freq
    ang = jnp.concatenate([ang_r, ang_r, ang_c, ang_c], axis=-1)
    return jnp.cos(ang), jnp.sin(ang)


def apply_axial_rope(x, cos, sin):
    x1, x2, x3, x4 = jnp.split(x, 4, axis=-1)
    rot = jnp.concatenate([-x2, x1, -x4, x3], axis=-1)
    y = x.astype(jnp.float32) * cos[None, :, None, :] + rot.astype(jnp.float32) * sin[None, :, None, :]
    return y.astype(x.dtype)


def attention_branch(q, k, v, q_norm_w, k_norm_w, out_norm_w):
    B, S = q.shape[0], q.shape[1]
    G = N_Q_HEADS // N_KV_HEADS
    q = rms_norm(q.reshape(B, S, N_Q_HEADS, HEAD_DIM), q_norm_w)
    k = rms_norm(k.reshape(B, S, N_KV_HEADS, HEAD_DIM), k_norm_w)
    v = v.reshape(B, S, N_KV_HEADS, HEAD_DIM)
    cos, sin = axial_rope_tables(S)
    q = apply_axial_rope(q, cos, sin)
    k = apply_axial_rope(k, cos, sin)
    nb = S // Q_BLOCK
    qb = q.reshape(B, nb, Q_BLOCK, N_KV_HEADS, G, HEAD_DIM).transpose(1, 0, 3, 4, 2, 5)
    kt = k.transpose(0, 2, 1, 3)
    vt = v.transpose(0, 2, 1, 3)
    scale = HEAD_DIM ** -0.5

    def block(q_blk):
        s = jnp.einsum('bkgqd,bksd->bkgqs', q_blk, kt).astype(jnp.float32) * scale
        p = jax.nn.softmax(s, axis=-1)
        return jnp.einsum('bkgqs,bksd->bkgqd', p.astype(vt.dtype), vt)

    o = lax.map(block, qb)
    o = o.transpose(1, 0, 4, 2, 3, 5).reshape(B, S, N_Q_HEADS, HEAD_DIM)
    o = rms_norm(o, out_norm_w.reshape(N_Q_HEADS, HEAD_DIM))
    return o.reshape(B, S, ATT_WIDTH)


def hgrn2_chunk_scan(q, log_f, k, v):
    B, S, H, DK = q.shape
    DV = v.shape[-1]
    N = S // HG_CHUNK

    def to_chunks(t):
        return t.reshape(B, N, HG_CHUNK, H, t.shape[-1]).transpose(1, 0, 3, 2, 4)

    causal = jnp.tril(jnp.ones((HG_CHUNK, HG_CHUNK), dtype=bool))

    def step(state, inp):
        qi, lf, ki, vi = inp
        b = jnp.cumsum(lf, axis=2)
        diff = b[:, :, :, None, :] - b[:, :, None, :, :]
        decay = jnp.exp(jnp.where(causal[:, :, None], diff, -jnp.inf))
        scores = jnp.einsum('bhik,bhjk,bhijk->bhij', qi, ki, decay)
        o = jnp.einsum('bhij,bhjv->bhiv', scores, vi) \
            + jnp.einsum('bhik,bhkv->bhiv', qi * jnp.exp(b), state)
        b_last = b[:, :, -1:, :]
        state = jnp.exp(b_last[:, :, 0, :])[..., None] * state \
            + jnp.einsum('bhjk,bhjv->bhkv', ki * jnp.exp(b_last - b), vi)
        return state, o

    s0 = jnp.zeros((B, H, DK, DV), jnp.float32)
    _, o = lax.scan(step, s0, (to_chunks(q), to_chunks(log_f), to_chunks(k), to_chunks(v)))
    return o.transpose(1, 0, 3, 2, 4).reshape(B, S, H, DV)


def hgrn2_branch(hq, hf_fwd, hf_bwd, hi, hg, lb_fwd, lb_bwd, out_norm_w):
    B, S = hq.shape[0], hq.shape[1]

    def heads(t):
        return t.reshape(B, S, HG_HEADS, -1).astype(jnp.float32)

    q = jax.nn.silu(heads(hq))
    v = heads(hi)

    def gates(z, lb):
        z = heads(z)
        lb = lb.reshape(HG_HEADS, HG_KDIM)
        f = lb + (1.0 - lb) * jax.nn.sigmoid(z)
        k = (1.0 - lb) * jax.nn.sigmoid(-z)
        return jnp.log(f), k

    lf_f, k_f = gates(hf_fwd, lb_fwd)
    lf_b, k_b = gates(hf_bwd, lb_bwd)
    o_fwd = hgrn2_chunk_scan(q, lf_f, k_f, v)
    o_bwd = jnp.flip(hgrn2_chunk_scan(jnp.flip(q, 1), jnp.flip(lf_b, 1), jnp.flip(k_b, 1), jnp.flip(v, 1)), 1)
    o = rms_norm(o_fwd + o_bwd, out_norm_w.reshape(HG_HEADS, HG_VDIM)) * jax.nn.silu(heads(hg))
    return o.reshape(B, S, HG_WIDTH).astype(hq.dtype)


def moe_ffn(h, w_router, router_bias, w_eg, w_eu, w_ed, w_sg, w_su, w_sd):
    B, S, D = h.shape
    T = B * S
    x = h.reshape(T, D)
    scores = jax.nn.sigmoid((x @ w_router).astype(jnp.float32))
    biased = scores + router_bias.astype(jnp.float32)
    grouped = biased.reshape(T, N_GROUPS, EXPERTS_PER_GROUP)
    group_score = jnp.sum(lax.top_k(grouped, 2)[0], axis=-1)
    _, top_groups = lax.top_k(group_score, TOPK_GROUPS)
    group_mask = jnp.sum(jax.nn.one_hot(top_groups, N_GROUPS, dtype=jnp.float32), axis=-2)
    expert_mask = jnp.repeat(group_mask, EXPERTS_PER_GROUP, axis=-1) > 0
    _, idx = lax.top_k(jnp.where(expert_mask, biased, -jnp.inf), TOP_K)
    w = jnp.take_along_axis(scores, idx, axis=-1)
    w = w / jnp.sum(w, axis=-1, keepdims=True) * ROUTED_SCALE
    gate = jnp.einsum('tk,tke->te', w, jax.nn.one_hot(idx, N_EXPERTS, dtype=jnp.float32))

    def expert_step(acc, inp):
        wg, wu, wd, g = inp
        hid = jax.nn.silu(x @ wg) * (x @ wu)
        return acc + g[:, None] * (hid @ wd).astype(jnp.float32), None

    routed, _ = lax.scan(expert_step, jnp.zeros((T, D), jnp.float32), (w_eg, w_eu, w_ed, gate.T))
    shared = (jax.nn.silu(x @ w_sg) * (x @ w_su)) @ w_sd
    return (routed + shared.astype(jnp.float32)).astype(h.dtype).reshape(B, S, D)


def split_points():
    pts, acc = [], 0
    for s in IN_SPLIT_SIZES[:-1]:
        acc += s
        pts.append(acc)
    return pts


def encoder_trunk(x, w_in, q_norm_w, k_norm_w, attn_out_norm_w, hgrn_lb_logits, hgrn_out_norm_w,
                  w_out, ln1_w, ln1_b, w_router, router_bias, w_exp_gate, w_exp_up, w_exp_down,
                  w_sh_gate, w_sh_up, w_sh_down, ln2_w, ln2_b):
    lb_all = jnp.cumsum(jax.nn.softmax(hgrn_lb_logits.astype(jnp.float32), axis=0), axis=0)
    pts = split_points()
    for layer in range(DEPTH):
        proj = x @ w_in[layer]
        q, k, v, hq, hf_f, hf_b, hi, hg = jnp.split(proj, pts, axis=-1)
        att = attention_branch(q, k, v, q_norm_w[layer], k_norm_w[layer], attn_out_norm_w[layer])
        hgo = hgrn2_branch(hq, hf_f, hf_b, hi, hg, lb_all[layer, 0], lb_all[layer, 1], hgrn_out_norm_w[layer])
        mix = jnp.concatenate([att, hgo], axis=-1) @ w_out[layer]
        x = layer_norm(DEEPNORM_ALPHA * x + mix, ln1_w[layer], ln1_b[layer])
        ffn = moe_ffn(x, w_router[layer], router_bias[layer], w_exp_gate[layer], w_exp_up[layer],
                      w_exp_down[layer], w_sh_gate[layer], w_sh_up[layer], w_sh_down[layer])
        x = layer_norm(DEEPNORM_ALPHA * x + ffn, ln2_w[layer], ln2_b[layer])
    return x


def setup_inputs(seed: int = 0) -> dict:
    key = jax.random.key(seed)
    ks = jax.random.split(key, 24)
    f32 = jnp.float32

    def nrm(k, shape, scale):
        return jax.random.normal(k, shape, f32) * scale

    def gain(k, shape):
        return 1.0 + 0.02 * jax.random.normal(k, shape, f32)

    D = D_MODEL
    return {
        'x_prompt': nrm(ks[0], (BATCH, SEQ, D), 1.0),
        'x_sample': nrm(ks[1], (DEC_BATCH, DEC_SEQ, D), 1.0),
        'w_in': nrm(ks[2], (DEPTH, D, IN_WIDTH), D ** -0.5),
        'q_norm_w': gain(ks[3], (DEPTH, HEAD_DIM)),
        'k_norm_w': gain(ks[4], (DEPTH, HEAD_DIM)),
        'attn_out_norm_w': gain(ks[5], (DEPTH, ATT_WIDTH)),
        'hgrn_lb_logits': nrm(ks[6], (DEPTH + 1, 2, HG_WIDTH), 0.1),
        'hgrn_out_norm_w': gain(ks[7], (DEPTH, HG_WIDTH)),
        'w_out': nrm(ks[8], (DEPTH, MIX_WIDTH, D), MIX_WIDTH ** -0.5 * DEEPNORM_BETA),
        'ln1_w': gain(ks[9], (DEPTH, D)),
        'ln1_b': nrm(ks[10], (DEPTH, D), 0.02),
        'w_router': nrm(ks[11], (DEPTH, D, N_EXPERTS), D ** -0.5),
        'router_bias': nrm(ks[12], (DEPTH, N_EXPERTS), 0.01),
        'w_exp_gate': nrm(ks[13], (DEPTH, N_EXPERTS, D, EXPERT_HIDDEN), D ** -0.5),
        'w_exp_up': nrm(ks[14], (DEPTH, N_EXPERTS, D, EXPERT_HIDDEN), D ** -0.5),
        'w_exp_down': nrm(ks[15], (DEPTH, N_EXPERTS, EXPERT_HIDDEN, D), EXPERT_HIDDEN ** -0.5 * DEEPNORM_BETA),
        'w_sh_gate': nrm(ks[16], (DEPTH, D, SHARED_HIDDEN), D ** -0.5),
        'w_sh_up': nrm(ks[17], (DEPTH, D, SHARED_HIDDEN), D ** -0.5),
        'w_sh_down': nrm(ks[18], (DEPTH, SHARED_HIDDEN, D), SHARED_HIDDEN ** -0.5 * DEEPNORM_BETA),
        'ln2_w': gain(ks[19], (DEPTH, D)),
        'ln2_b': nrm(ks[20], (DEPTH, D), 0.02),
    }


def reference(x_prompt, x_sample, w_in, q_norm_w, k_norm_w, attn_out_norm_w, hgrn_lb_logits,
              hgrn_out_norm_w, w_out, ln1_w, ln1_b, w_router, router_bias, w_exp_gate, w_exp_up,
              w_exp_down, w_sh_gate, w_sh_up, w_sh_down, ln2_w, ln2_b):
    y_prompt = encoder_trunk(x_prompt, w_in, q_norm_w, k_norm_w, attn_out_norm_w, hgrn_lb_logits,
                             hgrn_out_norm_w, w_out, ln1_w, ln1_b, w_router, router_bias, w_exp_gate,
                             w_exp_up, w_exp_down, w_sh_gate, w_sh_up, w_sh_down, ln2_w, ln2_b)
    y_sample = encoder_trunk(x_sample, w_in, q_norm_w, k_norm_w, attn_out_norm_w, hgrn_lb_logits,
                             hgrn_out_norm_w, w_out, ln1_w, ln1_b, w_router, router_bias, w_exp_gate,
                             w_exp_up, w_exp_down, w_sh_gate, w_sh_up, w_sh_down, ln2_w, ln2_b)
    return (y_prompt, y_sample)
```

```python
import functools

import jax
import jax.numpy as jnp
from jax import lax
from jax.experimental import pallas as pl
from jax.experimental.pallas import tpu as pltpu

F32 = jnp.float32
BF16 = jnp.bfloat16

D_MODEL = 1024
N_Q_HEADS = 8
N_KV_HEADS = 2
HEAD_DIM = 64
Q_PER_KV = N_Q_HEADS // N_KV_HEADS
ATT_WIDTH = N_Q_HEADS * HEAD_DIM
KV_WIDTH = N_KV_HEADS * HEAD_DIM
ROPE_THETA = 10000.0
ROPE_AXIS_DIM = HEAD_DIM // 2
ROPE_HALF = ROPE_AXIS_DIM // 2
GRID_W = 64
HG_HEADS = 8
HG_DIM = 64
HG_WIDTH = HG_HEADS * HG_DIM
HG_BLOCK = 128
HG_PAIRS = HG_WIDTH // HG_BLOCK
N_EXPERTS = 64
N_GROUPS = 8
EXPERTS_PER_GROUP = N_EXPERTS // N_GROUPS
TOPK_GROUPS = 4
TOP_K = 8
EXPERT_HIDDEN = 256
ROUTED_SCALE = 2.5
DEPTH = 1
DEEPNORM_ALPHA = (2.0 * DEPTH) ** 0.25
RMS_EPS = 1e-6
LN_EPS = 1e-5

_OFF_Q = 0
_OFF_K = _OFF_Q + ATT_WIDTH
_OFF_V = _OFF_K + KV_WIDTH
_OFF_HQ = _OFF_V + KV_WIDTH
_OFF_HFF = _OFF_HQ + HG_WIDTH
_OFF_HFB = _OFF_HFF + HG_WIDTH
_OFF_HI = _OFF_HFB + HG_WIDTH
_OFF_HG = _OFF_HI + HG_WIDTH
IN_WIDTH = _OFF_HG + HG_WIDTH

VMEM_LIMIT_BYTES = 56 * 1024 * 1024
LANES = 128

PROJ_ROWS = 512
ATT_Q_ROWS = 512
ATT_K_ROWS = 512
HG_ROWS = 256
MIX_ROWS = 512
MOE_ROWS = 1024


def _dot(a, b):
    return jnp.dot(a, b, preferred_element_type=F32)


def _dot_nt(a, b):
    return lax.dot_general(a, b, (((1,), (1,)), ((), ())), preferred_element_type=F32)


def _split_bf16(x):
    hi = x.astype(BF16)
    lo = (x - hi.astype(F32)).astype(BF16)
    return hi, lo


def _sigmoid(x):
    return 1.0 / (1.0 + jnp.exp(-x))


def _group_mean_sq(x, ones_blockdiag):
    hi, lo = _split_bf16(x * x)
    return (_dot(hi, ones_blockdiag) + _dot(lo, ones_blockdiag)) * (1.0 / HEAD_DIM)


def _layer_norm(h, w, b):
    mu = jnp.mean(h, axis=-1, keepdims=True)
    hc = h - mu
    var = jnp.mean(hc * hc, axis=-1, keepdims=True)
    return hc * lax.rsqrt(var + LN_EPS) * w + b


def _rope(x, cos, sin_lo, sin_hi):
    n = x.shape[1]
    reps = n // LANES
    cos = jnp.concatenate([cos] * reps, axis=1)
    sin_lo = jnp.concatenate([sin_lo] * reps, axis=1)
    sin_hi = jnp.concatenate([sin_hi] * reps, axis=1)
    return (x * cos + pltpu.roll(x, n - ROPE_HALF, 1) * sin_lo
            + pltpu.roll(x, ROPE_HALF, 1) * sin_hi)


def _proj_kernel(x_ref, w_ref, cos_ref, sinlo_ref, sinhi_ref, qw_ref, kw_ref, lb_ref, ones_ref,
                 qt_ref, k_ref, vt_ref, hq_ref, lff_ref, kf_ref, lfb_ref, kb_ref, hv_ref, hg_ref):
    xb = x_ref[...].astype(BF16)

    def seg(off, width):
        return _dot(xb, w_ref[:, off:off + width])

    cos, sin_lo, sin_hi = cos_ref[...], sinlo_ref[...], sinhi_ref[...]

    q = seg(_OFF_Q, ATT_WIDTH)
    q = q * lax.rsqrt(_group_mean_sq(q, ones_ref[...]) + RMS_EPS) * qw_ref[...]
    q = _rope(q, cos, sin_lo, sin_hi) * (HEAD_DIM ** -0.5)
    qt_ref[...] = q.T.astype(BF16)

    k = seg(_OFF_K, KV_WIDTH)
    k = k * lax.rsqrt(_group_mean_sq(k, ones_ref[0:KV_WIDTH, 0:KV_WIDTH]) + RMS_EPS) * kw_ref[...]
    k_ref[...] = _rope(k, cos, sin_lo, sin_hi).astype(BF16)

    vt_ref[...] = seg(_OFF_V, KV_WIDTH).T.astype(BF16)

    hq = seg(_OFF_HQ, HG_WIDTH)
    hq_ref[...] = (hq * _sigmoid(hq)).astype(BF16)

    for off, row, lf_ref, kk_ref in ((_OFF_HFF, 0, lff_ref, kf_ref), (_OFF_HFB, 1, lfb_ref, kb_ref)):
        z = seg(off, HG_WIDTH)
        lb = lb_ref[row:row + 1, :]
        lf_ref[...] = jnp.log(lb + (1.0 - lb) * _sigmoid(z))
        kk_ref[...] = ((1.0 - lb) * _sigmoid(-z)).astype(BF16)

    hv_ref[...] = seg(_OFF_HI, HG_WIDTH).astype(BF16)
    hg = seg(_OFF_HG, HG_WIDTH)
    hg_ref[...] = (hg * _sigmoid(hg)).astype(BF16)


def _rope_tables(seq):
    pos = jnp.arange(seq, dtype=jnp.int32)
    row_ids = (pos // GRID_W).astype(F32)
    col_ids = (pos % GRID_W).astype(F32)
    inv_freq = ROPE_THETA ** (-jnp.arange(0, ROPE_AXIS_DIM, 2, dtype=F32) / ROPE_AXIS_DIM)
    ang_r = row_ids[:, None] * inv_freq
    ang_c = col_ids[:, None] * inv_freq
    ang = jnp.concatenate([ang_r, ang_r, ang_c, ang_c], axis=-1)
    ang = jnp.concatenate([ang] * (LANES // HEAD_DIM), axis=-1)
    lower = (jnp.arange(LANES) % ROPE_AXIS_DIM) < ROPE_HALF
    sin = jnp.sin(ang)
    return jnp.cos(ang), jnp.where(lower, -sin, 0.0), jnp.where(lower, 0.0, sin)


def _project(x2, batch, seq, w_in, q_norm_w, k_norm_w, lb, ones_blockdiag):
    t = batch * seq
    tm = PROJ_ROWS
    nt = seq // tm
    cos, sin_lo, sin_hi = _rope_tables(seq)
    qw = jnp.tile(q_norm_w, N_Q_HEADS)[None, :]
    kw = jnp.tile(k_norm_w, N_KV_HEADS)[None, :]

    row_block = lambda w: pl.BlockSpec((tm, w), lambda i: (i, 0))
    pos_block = pl.BlockSpec((tm, LANES), lambda i: (i % nt, 0))
    full = lambda a: pl.BlockSpec(a.shape, lambda i: (0,) * a.ndim)
    t_block = lambda w: pl.BlockSpec((None, w, tm), lambda i: (i // nt, 0, i % nt))

    hg_bf16 = jax.ShapeDtypeStruct((t, HG_WIDTH), BF16)
    hg_f32 = jax.ShapeDtypeStruct((t, HG_WIDTH), F32)
    return pl.pallas_call(
        _proj_kernel,
        grid=(t // tm,),
        in_specs=[row_block(D_MODEL), full(w_in), pos_block, pos_block, pos_block,
                  full(qw), full(kw), full(lb), full(ones_blockdiag)],
        out_specs=[t_block(ATT_WIDTH), row_block(KV_WIDTH), t_block(KV_WIDTH),
                   row_block(HG_WIDTH), row_block(HG_WIDTH), row_block(HG_WIDTH),
                   row_block(HG_WIDTH), row_block(HG_WIDTH), row_block(HG_WIDTH), row_block(HG_WIDTH)],
        out_shape=[jax.ShapeDtypeStruct((batch, ATT_WIDTH, seq), BF16),
                   jax.ShapeDtypeStruct((t, KV_WIDTH), BF16),
                   jax.ShapeDtypeStruct((batch, KV_WIDTH, seq), BF16),
                   hg_bf16, hg_f32, hg_bf16, hg_f32, hg_bf16, hg_bf16, hg_bf16],
        compiler_params=pltpu.CompilerParams(dimension_semantics=("parallel",),
                                             vmem_limit_bytes=VMEM_LIMIT_BYTES),
        name="proj",
    )(x2, w_in, cos, sin_lo, sin_hi, qw, kw, lb, ones_blockdiag)


def _attn_kernel(qt_ref, k_ref, vt_ref, nw_ref, o_ref, qz_ref, m_ref, l_ref, acc_ref):
    ki = pl.program_id(2)
    tq = qt_ref.shape[1]

    @pl.when(ki == 0)
    def _init():
        zeros = jnp.zeros((HEAD_DIM, tq), BF16)
        for h in range(N_Q_HEADS):
            qh = qt_ref[h * HEAD_DIM:(h + 1) * HEAD_DIM, :]
            qz_ref[h] = jnp.concatenate([qh, zeros] if h < Q_PER_KV else [zeros, qh], axis=0)
        m_ref[...] = jnp.full(m_ref.shape, -jnp.inf, F32)
        l_ref[...] = jnp.zeros(l_ref.shape, F32)
        acc_ref[...] = jnp.zeros(acc_ref.shape, F32)

    k = k_ref[...]
    for h in range(N_Q_HEADS):
        kv = h // Q_PER_KV
        s = _dot(k, qz_ref[h])
        m_old = m_ref[h]
        m_new = jnp.maximum(m_old, jnp.max(s, axis=0, keepdims=True))
        alpha = jnp.exp(m_old - m_new)
        p = jnp.exp(s - m_new)
        l_ref[h] = alpha * l_ref[h] + jnp.sum(p, axis=0, keepdims=True)
        vt = vt_ref[kv * HEAD_DIM:(kv + 1) * HEAD_DIM, :]
        acc_ref[h] = alpha * acc_ref[h] + _dot(vt, p.astype(BF16))
        m_ref[h] = m_new

    @pl.when(ki == pl.num_programs(2) - 1)
    def _finish():
        outs = []
        for h in range(N_Q_HEADS):
            o = acc_ref[h] / l_ref[h]
            ms = jnp.mean(o * o, axis=0, keepdims=True)
            outs.append(o * lax.rsqrt(ms + RMS_EPS) * nw_ref[h * HEAD_DIM:(h + 1) * HEAD_DIM, :])
        o_ref[...] = jnp.concatenate(outs, axis=0).T


def _attention(qt, k, vt, out_norm_w, batch, seq):
    tq, tk = ATT_Q_ROWS, ATT_K_ROWS
    nq, nk = seq // tq, seq // tk
    nw = out_norm_w[:, None]
    return pl.pallas_call(
        _attn_kernel,
        grid=(batch, nq, nk),
        in_specs=[pl.BlockSpec((None, ATT_WIDTH, tq), lambda b, qi, ki: (b, 0, qi)),
                  pl.BlockSpec((tk, KV_WIDTH), lambda b, qi, ki: (b * nk + ki, 0)),
                  pl.BlockSpec((None, KV_WIDTH, tk), lambda b, qi, ki: (b, 0, ki)),
                  pl.BlockSpec(nw.shape, lambda b, qi, ki: (0, 0))],
        out_specs=pl.BlockSpec((tq, ATT_WIDTH), lambda b, qi, ki: (b * nq + qi, 0)),
        out_shape=jax.ShapeDtypeStruct((batch * seq, ATT_WIDTH), F32),
        scratch_shapes=[pltpu.VMEM((N_Q_HEADS, KV_WIDTH, tq), BF16),
                        pltpu.VMEM((N_Q_HEADS, 1, tq), F32),
                        pltpu.VMEM((N_Q_HEADS, 1, tq), F32),
                        pltpu.VMEM((N_Q_HEADS, HEAD_DIM, tq), F32)],
        compiler_params=pltpu.CompilerParams(
            dimension_semantics=("parallel", "parallel", "arbitrary"),
            vmem_limit_bytes=VMEM_LIMIT_BYTES),
        name="attention",
    )(qt, k, vt, nw)


def _hgrn_block(qh, lf, kk, v, st_ref, reverse):
    n = HG_BLOCK
    row = lax.broadcasted_iota(jnp.int32, (n, n), 0)
    col = lax.broadcasted_iota(jnp.int32, (n, n), 1)
    tri = (col >= row) if reverse else (col <= row)
    tri_bf16 = jnp.where(tri, 1.0, 0.0).astype(BF16)
    first_head = col < HG_DIM
    same_head = (row < HG_DIM) == first_head

    lf_hi, lf_lo = _split_bf16(lf)
    b = _dot(tri_bf16, lf_hi) + _dot(tri_bf16, lf_lo)
    if reverse:
        b_end, b_mid = b[0:1, :], b[n // 2:n // 2 + 1, :]
    else:
        b_end, b_mid = b[n - 1:n, :], b[n // 2 - 1:n // 2, :]

    qf = qh.astype(F32)
    kf = kk.astype(F32)
    q_in = (qf * jnp.exp(b - b_mid)).astype(BF16)
    k_in = (kf * jnp.exp(b_mid - b)).astype(BF16)
    q_st = (qf * jnp.exp(b)).astype(BF16)
    k_st = (kf * jnp.exp(b_end - b)).astype(BF16)
    decay = jnp.exp(b_end)

    zero = jnp.zeros((), BF16)
    outs = []
    for p in range(HG_PAIRS):
        sl = slice(p * n, (p + 1) * n)
        q_p, k_p, v_p = q_in[:, sl], k_in[:, sl], v[:, sl]
        a0 = jnp.where(tri, _dot_nt(jnp.where(first_head, q_p, zero), k_p), 0.0).astype(BF16)
        a1 = jnp.where(tri, _dot_nt(jnp.where(first_head, zero, q_p), k_p), 0.0).astype(BF16)
        st = st_ref[p]
        o_p = (_dot(a0, jnp.where(first_head, v_p, zero)) + _dot(a1, jnp.where(first_head, zero, v_p))
               + _dot_nt(q_st[:, sl], st.astype(BF16)))
        outs.append(o_p)
        upd = _dot(v_p.astype(F32).T.astype(BF16), k_st[:, sl])
        st_ref[p] = st * decay[:, sl] + jnp.where(same_head, upd, 0.0)
    return jnp.concatenate(outs, axis=1)


def _hgrn_kernel(qf_ref, lff_ref, kf_ref, vf_ref, qb_ref, lfb_ref, kb_ref, vb_ref,
                 of_ref, ob_ref, stf_ref, stb_ref):
    @pl.when(pl.program_id(1) == 0)
    def _init():
        stf_ref[...] = jnp.zeros(stf_ref.shape, F32)
        stb_ref[...] = jnp.zeros(stb_ref.shape, F32)

    nblk = qf_ref.shape[0] // HG_BLOCK
    for j in range(nblk):
        rows = slice(j * HG_BLOCK, (j + 1) * HG_BLOCK)
        of_ref[rows, :] = _hgrn_block(qf_ref[rows, :], lff_ref[rows, :], kf_ref[rows, :],
                                      vf_ref[rows, :], stf_ref, reverse=False)
    for j in reversed(range(nblk)):
        rows = slice(j * HG_BLOCK, (j + 1) * HG_BLOCK)
        ob_ref[rows, :] = _hgrn_block(qb_ref[rows, :], lfb_ref[rows, :], kb_ref[rows, :],
                                      vb_ref[rows, :], stb_ref, reverse=True)


def _hgrn(hq, lff, kf, lfb, kb, hv, batch, seq):
    tc = HG_ROWS
    nt = seq // tc
    fwd = pl.BlockSpec((tc, HG_WIDTH), lambda b, n: (b * nt + n, 0))
    bwd = pl.BlockSpec((tc, HG_WIDTH), lambda b, n: (b * nt + nt - 1 - n, 0))
    out = jax.ShapeDtypeStruct((batch * seq, HG_WIDTH), F32)
    state = pltpu.VMEM((HG_PAIRS, HG_BLOCK, HG_BLOCK), F32)
    return pl.pallas_call(
        _hgrn_kernel,
        grid=(batch, nt),
        in_specs=[fwd, fwd, fwd, fwd, bwd, bwd, bwd, bwd],
        out_specs=[fwd, bwd],
        out_shape=[out, out],
        scratch_shapes=[state, state],
        compiler_params=pltpu.CompilerParams(dimension_semantics=("parallel", "arbitrary"),
                                             vmem_limit_bytes=VMEM_LIMIT_BYTES),
        name="hgrn",
    )(hq, lff, kf, hv, hq, lfb, kb, hv)


def _route(scores, biased):
    tm = scores.shape[1]
    neg_inf = -jnp.inf
    group_scores = []
    for g in range(N_GROUPS):
        blk = biased[g * EXPERTS_PER_GROUP:(g + 1) * EXPERTS_PER_GROUP, :]
        m1 = jnp.max(blk, axis=0, keepdims=True)
        n_max = jnp.sum(jnp.where(blk == m1, 1.0, 0.0), axis=0, keepdims=True)
        below = jnp.max(jnp.where(blk < m1, blk, neg_inf), axis=0, keepdims=True)
        group_scores.append(m1 + jnp.where(n_max >= 2.0, m1, below))
    gs = jnp.concatenate(group_scores, axis=0)

    gidx = lax.broadcasted_iota(jnp.int32, (N_GROUPS, tm), 0)
    rank = jnp.zeros((N_GROUPS, tm), F32)
    for g in range(N_GROUPS):
        other = gs[g:g + 1, :]
        ahead = jnp.where(other > gs, 1.0, jnp.where(other == gs, jnp.where(gidx > g, 1.0, 0.0), 0.0))
        rank = rank + ahead
    group_on = jnp.where(rank < float(TOPK_GROUPS), 1.0, 0.0)
    expert_on = jnp.concatenate(
        [jnp.broadcast_to(group_on[g:g + 1, :], (EXPERTS_PER_GROUP, tm)) for g in range(N_GROUPS)], axis=0)

    eidx = lax.broadcasted_iota(jnp.int32, (N_EXPERTS, tm), 0).astype(F32)
    work = jnp.where(expert_on > 0.0, biased, neg_inf)
    chosen = jnp.zeros((N_EXPERTS, tm), F32)
    for _ in range(TOP_K):
        best = jnp.max(work, axis=0, keepdims=True)
        first = jnp.min(jnp.where(work == best, eidx, float(N_EXPERTS)), axis=0, keepdims=True)
        pick = eidx == first
        chosen = jnp.where(pick, 1.0, chosen)
        work = jnp.where(pick, neg_inf, work)
    w = jnp.where(chosen > 0.0, scores, 0.0)
    return w / jnp.sum(w, axis=0, keepdims=True) * ROUTED_SCALE


def _mix_kernel(x_ref, att_ref, of_ref, ob_ref, g_ref, wout_ref, hnw_ref, lnw_ref, lnb_ref,
                wrhi_ref, wrlo_ref, rb_ref, ones_ref, x1_ref, gate_ref):
    o = of_ref[...] + ob_ref[...]
    o = o * lax.rsqrt(_group_mean_sq(o, ones_ref[...]) + RMS_EPS) * hnw_ref[...] * g_ref[...].astype(F32)
    mixed = jnp.concatenate([att_ref[...].astype(BF16), o.astype(BF16)], axis=1)
    h = DEEPNORM_ALPHA * x_ref[...] + _dot(mixed, wout_ref[...])
    x1 = _layer_norm(h, lnw_ref[...], lnb_ref[...])
    x1_ref[...] = x1

    x_hi, x_lo = _split_bf16(x1)
    w_hi, w_lo = wrhi_ref[...], wrlo_ref[...]
    logits = _dot_nt(w_hi, x_hi) + _dot_nt(w_hi, x_lo) + _dot_nt(w_lo, x_hi)
    scores = _sigmoid(logits)
    gate = _route(scores, scores + rb_ref[...])
    gate = jnp.concatenate([gate, jnp.zeros_like(gate)], axis=0)
    gate_ref[...] = gate.T


def _mix(x2, att, o_f, o_b, hg, w_out, hg_norm_w, ln_w, ln_b, w_router, router_bias, ones_blockdiag):
    t = x2.shape[0]
    tm = MIX_ROWS
    wr_hi, wr_lo = _split_bf16(w_router.T)
    args = (x2, att, o_f, o_b, hg, w_out, hg_norm_w[None, :], ln_w[None, :], ln_b[None, :],
            wr_hi, wr_lo, router_bias[:, None], ones_blockdiag)
    row_block = lambda w: pl.BlockSpec((tm, w), lambda i: (i, 0))
    full = lambda a: pl.BlockSpec(a.shape, lambda i: (0,) * a.ndim)
    return pl.pallas_call(
        _mix_kernel,
        grid=(t // tm,),
        in_specs=[row_block(D_MODEL), row_block(ATT_WIDTH), row_block(HG_WIDTH), row_block(HG_WIDTH),
                  row_block(HG_WIDTH)] + [full(a) for a in args[5:]],
        out_specs=[row_block(D_MODEL), row_block(LANES)],
        out_shape=[jax.ShapeDtypeStruct((t, D_MODEL), F32), jax.ShapeDtypeStruct((t, LANES), F32)],
        compiler_params=pltpu.CompilerParams(dimension_semantics=("parallel",),
                                             vmem_limit_bytes=VMEM_LIMIT_BYTES),
        name="mix",
    )(*args)


def _swiglu(xb, w_gate, w_up):
    hg = _dot(xb, w_gate)
    return hg * _sigmoid(hg) * _dot(xb, w_up)


def _moe_kernel(x1_ref, gate_ref, wg_ref, wu_ref, wd_ref, wsg_ref, wsu_ref, wsd_ref, lnw_ref, lnb_ref,
                o_ref, xb_ref, acc_ref):
    e = pl.program_id(1)

    @pl.when(e == 0)
    def _init():
        xb = x1_ref[...].astype(BF16)
        xb_ref[...] = xb
        acc_ref[...] = _dot(_swiglu(xb, wsg_ref[...], wsu_ref[...]).astype(BF16), wsd_ref[...])

    xb = xb_ref[...]
    hid = _swiglu(xb, wg_ref[...], wu_ref[...])
    lane = lax.broadcasted_iota(jnp.int32, gate_ref.shape, 1)
    g = jnp.sum(jnp.where(lane == e, gate_ref[...], 0.0), axis=1, keepdims=True)
    acc_ref[...] += _dot((hid * g).astype(BF16), wd_ref[...])

    @pl.when(e == pl.num_programs(1) - 1)
    def _finish():
        h = DEEPNORM_ALPHA * x1_ref[...] + acc_ref[...]
        o_ref[...] = _layer_norm(h, lnw_ref[...], lnb_ref[...])


def _moe(x1, gate, w_eg, w_eu, w_ed, w_sg, w_su, w_sd, ln_w, ln_b):
    t = x1.shape[0]
    tm = MOE_ROWS
    args = (x1, gate, w_eg, w_eu, w_ed, w_sg, w_su, w_sd, ln_w[None, :], ln_b[None, :])
    row_block = lambda w: pl.BlockSpec((tm, w), lambda i, e: (i, 0))
    full = lambda a: pl.BlockSpec(a.shape, lambda i, e: (0,) * a.ndim)
    expert = lambda a: pl.BlockSpec((None,) + a.shape[1:], lambda i, e: (e, 0, 0))
    return pl.pallas_call(
        _moe_kernel,
        grid=(t // tm, N_EXPERTS),
        in_specs=[row_block(D_MODEL), row_block(LANES), expert(w_eg), expert(w_eu), expert(w_ed)]
                 + [full(a) for a in args[5:]],
        out_specs=row_block(D_MODEL),
        out_shape=jax.ShapeDtypeStruct((t, D_MODEL), F32),
        scratch_shapes=[pltpu.VMEM((tm, D_MODEL), BF16), pltpu.VMEM((tm, D_MODEL), F32)],
        compiler_params=pltpu.CompilerParams(dimension_semantics=("parallel", "arbitrary"),
                                             vmem_limit_bytes=VMEM_LIMIT_BYTES),
        name="moe",
    )(*args)


def _trunk(x, p):
    batch, seq, _ = x.shape
    x2 = x.reshape(batch * seq, D_MODEL)
    qt, k, vt, hq, lff, kf, lfb, kb, hv, hg = _project(
        x2, batch, seq, p["w_in"], p["q_norm_w"], p["k_norm_w"], p["lb"], p["ones"])
    att = _attention(qt, k, vt, p["attn_out_norm_w"], batch, seq)
    o_f, o_b = _hgrn(hq, lff, kf, lfb, kb, hv, batch, seq)
    x1, gate = _mix(x2, att, o_f, o_b, hg, p["w_out"], p["hgrn_out_norm_w"], p["ln1_w"], p["ln1_b"],
                    p["w_router"], p["router_bias"], p["ones"])
    y = _moe(x1, gate, p["w_exp_gate"], p["w_exp_up"], p["w_exp_down"], p["w_sh_gate"], p["w_sh_up"],
             p["w_sh_down"], p["ln2_w"], p["ln2_b"])
    return y.reshape(batch, seq, D_MODEL)


def _prepare(w_in, q_norm_w, k_norm_w, attn_out_norm_w, hgrn_lb_logits, hgrn_out_norm_w, w_out, ln1_w, ln1_b, w_router, router_bias, w_exp_gate, w_exp_up, w_exp_down, w_sh_gate, w_sh_up, w_sh_down, ln2_w, ln2_b):
    layer = 0
    lb_all = jnp.cumsum(jax.nn.softmax(hgrn_lb_logits.astype(F32), axis=0), axis=0)
    head_of = jnp.arange(ATT_WIDTH) // HEAD_DIM
    return {
        "w_in": w_in[layer].astype(BF16),
        "q_norm_w": q_norm_w[layer], "k_norm_w": k_norm_w[layer],
        "attn_out_norm_w": attn_out_norm_w[layer],
        "lb": lb_all[layer],
        "hgrn_out_norm_w": hgrn_out_norm_w[layer],
        "w_out": w_out[layer].astype(BF16),
        "ln1_w": ln1_w[layer], "ln1_b": ln1_b[layer],
        "w_router": w_router[layer], "router_bias": router_bias[layer],
        "w_exp_gate": w_exp_gate[layer].astype(BF16), "w_exp_up": w_exp_up[layer].astype(BF16),
        "w_exp_down": w_exp_down[layer].astype(BF16),
        "w_sh_gate": w_sh_gate[layer].astype(BF16), "w_sh_up": w_sh_up[layer].astype(BF16),
        "w_sh_down": w_sh_down[layer].astype(BF16),
        "ln2_w": ln2_w[layer], "ln2_b": ln2_b[layer],
        "ones": (head_of[:, None] == head_of[None, :]).astype(BF16),
    }


def kernel(x_prompt, x_sample, w_in, q_norm_w, k_norm_w, attn_out_norm_w, hgrn_lb_logits, hgrn_out_norm_w, w_out, ln1_w, ln1_b, w_router, router_bias, w_exp_gate, w_exp_up, w_exp_down, w_sh_gate, w_sh_up, w_sh_down, ln2_w, ln2_b):
    p = _prepare(w_in, q_norm_w, k_norm_w, attn_out_norm_w, hgrn_lb_logits, hgrn_out_norm_w, w_out, ln1_w, ln1_b, w_router, router_bias, w_exp_gate, w_exp_up, w_exp_down, w_sh_gate, w_sh_up, w_sh_down, ln2_w, ln2_b)
    return _trunk(x_prompt, p), _trunk(x_sample, p)
```

```python
import functools

import jax
import jax.numpy as jnp
from jax import lax
from jax.experimental import pallas as pl
from jax.experimental.pallas import tpu as pltpu

F32 = jnp.float32
BF16 = jnp.bfloat16

D_MODEL = 1024
N_Q_HEADS = 8
N_KV_HEADS = 2
HEAD_DIM = 64
Q_PER_KV = N_Q_HEADS // N_KV_HEADS
ATT_WIDTH = N_Q_HEADS * HEAD_DIM
KV_WIDTH = N_KV_HEADS * HEAD_DIM
ROPE_THETA = 10000.0
ROPE_AXIS_DIM = HEAD_DIM // 2
ROPE_HALF = ROPE_AXIS_DIM // 2
GRID_W = 64
HG_HEADS = 8
HG_DIM = 64
HG_WIDTH = HG_HEADS * HG_DIM
HG_BLOCK = 128
HG_PAIRS = HG_WIDTH // HG_BLOCK
N_EXPERTS = 64
N_GROUPS = 8
EXPERTS_PER_GROUP = N_EXPERTS // N_GROUPS
TOPK_GROUPS = 4
TOP_K = 8
EXPERT_HIDDEN = 256
ROUTED_SCALE = 2.5
DEPTH = 1
DEEPNORM_ALPHA = (2.0 * DEPTH) ** 0.25
RMS_EPS = 1e-6
LN_EPS = 1e-5

_OFF_Q = 0
_OFF_K = _OFF_Q + ATT_WIDTH
_OFF_V = _OFF_K + KV_WIDTH
_OFF_HQ = _OFF_V + KV_WIDTH
_OFF_HFF = _OFF_HQ + HG_WIDTH
_OFF_HFB = _OFF_HFF + HG_WIDTH
_OFF_HI = _OFF_HFB + HG_WIDTH
_OFF_HG = _OFF_HI + HG_WIDTH
IN_WIDTH = _OFF_HG + HG_WIDTH

VMEM_LIMIT_BYTES = 56 * 1024 * 1024
LANES = 128

LOG2_E = 1.4426950408889634
Q_SCALE = HEAD_DIM ** -0.5 * LOG2_E
VT_ROWS = HEAD_DIM + 16
ATT_MIN_SUM = 2.0 ** -80

PROJ_ROWS = 512
ATT_Q_ROWS = 1024
ATT_K_ROWS = 1024
ATT_SUB_K = 256
ATT_SUB_Q = 1024
ATT_SAFE_ROWS = 512
HG_ROWS = 256
MIX_ROWS = 512
MOE_ROWS = 1024


def _dot(a, b):
    return jnp.dot(a, b, preferred_element_type=F32)


def _dot_nt(a, b):
    return lax.dot_general(a, b, (((1,), (1,)), ((), ())), preferred_element_type=F32)


def _split_bf16(x):
    hi = x.astype(BF16)
    lo = (x - hi.astype(F32)).astype(BF16)
    return hi, lo


def _sigmoid(x):
    return 1.0 / (1.0 + jnp.exp(-x))


def _group_mean_sq(x, ones_blockdiag):
    hi, lo = _split_bf16(x * x)
    return (_dot(hi, ones_blockdiag) + _dot(lo, ones_blockdiag)) * (1.0 / HEAD_DIM)


def _layer_norm(h, w, b):
    mu = jnp.mean(h, axis=-1, keepdims=True)
    hc = h - mu
    var = jnp.mean(hc * hc, axis=-1, keepdims=True)
    return hc * lax.rsqrt(var + LN_EPS) * w + b


def _rope(x, cos, sin_lo, sin_hi):
    n = x.shape[1]
    reps = n // LANES
    cos = jnp.concatenate([cos] * reps, axis=1)
    sin_lo = jnp.concatenate([sin_lo] * reps, axis=1)
    sin_hi = jnp.concatenate([sin_hi] * reps, axis=1)
    return (x * cos + pltpu.roll(x, n - ROPE_HALF, 1) * sin_lo
            + pltpu.roll(x, ROPE_HALF, 1) * sin_hi)


def _proj_kernel(x_ref, w_ref, cos_ref, sinlo_ref, sinhi_ref, qw_ref, kw_ref, lb_ref, ones_ref,
                 qt_ref, k_ref, vt_ref, hq_ref, lff_ref, kf_ref, lfb_ref, kb_ref, hv_ref, hg_ref):
    xb = x_ref[...].astype(BF16)

    def seg(off, width):
        return _dot(xb, w_ref[:, off:off + width])

    cos, sin_lo, sin_hi = cos_ref[...], sinlo_ref[...], sinhi_ref[...]

    q = seg(_OFF_Q, ATT_WIDTH)
    q = q * lax.rsqrt(_group_mean_sq(q, ones_ref[...]) + RMS_EPS) * qw_ref[...]
    q = _rope(q, cos, sin_lo, sin_hi) * Q_SCALE
    qt_ref[...] = q.T.astype(BF16)

    k = seg(_OFF_K, KV_WIDTH)
    k = k * lax.rsqrt(_group_mean_sq(k, ones_ref[0:KV_WIDTH, 0:KV_WIDTH]) + RMS_EPS) * kw_ref[...]
    k_ref[...] = _rope(k, cos, sin_lo, sin_hi).astype(BF16)

    vt = seg(_OFF_V, KV_WIDTH).T
    ones_rows = jnp.ones((VT_ROWS - HEAD_DIM, vt.shape[1]), F32)
    vt_ref[...] = jnp.concatenate(
        [piece for kv in range(N_KV_HEADS) for piece in (vt[kv * HEAD_DIM:(kv + 1) * HEAD_DIM, :], ones_rows)],
        axis=0).astype(BF16)

    hq = seg(_OFF_HQ, HG_WIDTH)
    hq_ref[...] = (hq * _sigmoid(hq)).astype(BF16)

    for off, row, lf_ref, kk_ref in ((_OFF_HFF, 0, lff_ref, kf_ref), (_OFF_HFB, 1, lfb_ref, kb_ref)):
        z = seg(off, HG_WIDTH)
        lb = lb_ref[row:row + 1, :]
        lf_ref[...] = jnp.log(lb + (1.0 - lb) * _sigmoid(z))
        kk_ref[...] = ((1.0 - lb) * _sigmoid(-z)).astype(BF16)

    hv_ref[...] = seg(_OFF_HI, HG_WIDTH).astype(BF16)
    hg = seg(_OFF_HG, HG_WIDTH)
    hg_ref[...] = (hg * _sigmoid(hg)).astype(BF16)


def _rope_tables(seq):
    pos = jnp.arange(seq, dtype=jnp.int32)
    row_ids = (pos // GRID_W).astype(F32)
    col_ids = (pos % GRID_W).astype(F32)
    inv_freq = ROPE_THETA ** (-jnp.arange(0, ROPE_AXIS_DIM, 2, dtype=F32) / ROPE_AXIS_DIM)
    ang_r = row_ids[:, None] * inv_freq
    ang_c = col_ids[:, None] * inv_freq
    ang = jnp.concatenate([ang_r, ang_r, ang_c, ang_c], axis=-1)
    ang = jnp.concatenate([ang] * (LANES // HEAD_DIM), axis=-1)
    lower = (jnp.arange(LANES) % ROPE_AXIS_DIM) < ROPE_HALF
    sin = jnp.sin(ang)
    return jnp.cos(ang), jnp.where(lower, -sin, 0.0), jnp.where(lower, 0.0, sin)


def _project(x2, batch, seq, w_in, q_norm_w, k_norm_w, lb, ones_blockdiag):
    t = batch * seq
    tm = PROJ_ROWS
    nt = seq // tm
    cos, sin_lo, sin_hi = _rope_tables(seq)
    qw = jnp.tile(q_norm_w, N_Q_HEADS)[None, :]
    kw = jnp.tile(k_norm_w, N_KV_HEADS)[None, :]

    row_block = lambda w: pl.BlockSpec((tm, w), lambda i: (i, 0))
    pos_block = pl.BlockSpec((tm, LANES), lambda i: (i % nt, 0))
    full = lambda a: pl.BlockSpec(a.shape, lambda i: (0,) * a.ndim)
    t_block = lambda w: pl.BlockSpec((None, w, tm), lambda i: (i // nt, 0, i % nt))

    hg_bf16 = jax.ShapeDtypeStruct((t, HG_WIDTH), BF16)
    hg_f32 = jax.ShapeDtypeStruct((t, HG_WIDTH), F32)
    return pl.pallas_call(
        _proj_kernel,
        grid=(t // tm,),
        in_specs=[row_block(D_MODEL), full(w_in), pos_block, pos_block, pos_block,
                  full(qw), full(kw), full(lb), full(ones_blockdiag)],
        out_specs=[t_block(ATT_WIDTH), row_block(KV_WIDTH), t_block(N_KV_HEADS * VT_ROWS),
                   row_block(HG_WIDTH), row_block(HG_WIDTH), row_block(HG_WIDTH),
                   row_block(HG_WIDTH), row_block(HG_WIDTH), row_block(HG_WIDTH), row_block(HG_WIDTH)],
        out_shape=[jax.ShapeDtypeStruct((batch, ATT_WIDTH, seq), BF16),
                   jax.ShapeDtypeStruct((t, KV_WIDTH), BF16),
                   jax.ShapeDtypeStruct((batch, N_KV_HEADS * VT_ROWS, seq), BF16),
                   hg_bf16, hg_f32, hg_bf16, hg_f32, hg_bf16, hg_bf16, hg_bf16],
        compiler_params=pltpu.CompilerParams(dimension_semantics=("parallel",),
                                             vmem_limit_bytes=VMEM_LIMIT_BYTES),
        name="proj",
    )(x2, w_in, cos, sin_lo, sin_hi, qw, kw, lb, ones_blockdiag)


def _load_queries(qt_ref, qz_ref):
    tq = qt_ref.shape[1]
    zeros = jnp.zeros((HEAD_DIM, tq), BF16)
    for h in range(N_Q_HEADS):
        qh = qt_ref[h * HEAD_DIM:(h + 1) * HEAD_DIM, :]
        qz_ref[h] = jnp.concatenate([qh, zeros] if h < Q_PER_KV else [zeros, qh], axis=0)


def _normalise_heads(numer, denom, nw_ref):
    outs = []
    for h in range(N_Q_HEADS):
        o = numer(h) / denom(h)
        ms = jnp.mean(o * o, axis=0, keepdims=True)
        outs.append(o * lax.rsqrt(ms + RMS_EPS) * nw_ref[h * HEAD_DIM:(h + 1) * HEAD_DIM, :])
    return jnp.concatenate(outs, axis=0).T


def _attn_kernel(kmax_ref, qt_ref, k_ref, vt_ref, nw_ref, o_ref, l_ref, qz_ref, c_ref, acc_ref):
    ki = pl.program_id(2)
    tq = qt_ref.shape[1]
    tk = k_ref.shape[0]

    @pl.when(ki == 0)
    def _init():
        _load_queries(qt_ref, qz_ref)
        for h in range(N_Q_HEADS):
            qf = qt_ref[h * HEAD_DIM:(h + 1) * HEAD_DIM, :].astype(F32)
            c_ref[h] = jnp.sqrt(jnp.sum(qf * qf, axis=0, keepdims=True)) * kmax_ref[...]
        acc_ref[...] = jnp.zeros(acc_ref.shape, F32)

    def finish_tile(tile):
        s, h, keys, qsl = tile
        kv = h // Q_PER_KV
        p = jnp.exp2(s - c_ref[h, :, qsl]).astype(BF16)
        acc_ref[h, :, qsl] += _dot(vt_ref[kv * VT_ROWS:(kv + 1) * VT_ROWS, keys], p)

    pending = None
    for h in range(N_Q_HEADS):
        for js in range(tk // ATT_SUB_K):
            keys = slice(js * ATT_SUB_K, (js + 1) * ATT_SUB_K)
            for qs in range(tq // ATT_SUB_Q):
                qsl = slice(qs * ATT_SUB_Q, (qs + 1) * ATT_SUB_Q)
                s = _dot(k_ref[keys, :], qz_ref[h, :, qsl])
                if pending is not None:
                    finish_tile(pending)
                pending = (s, h, keys, qsl)
    finish_tile(pending)

    @pl.when(ki == pl.num_programs(2) - 1)
    def _finish():
        o_ref[...] = _normalise_heads(lambda h: acc_ref[h, 0:HEAD_DIM, :],
                                      lambda h: acc_ref[h, HEAD_DIM:HEAD_DIM + 1, :], nw_ref)
        for h in range(N_Q_HEADS):
            l_ref[h:h + 1, :] = acc_ref[h, HEAD_DIM:HEAD_DIM + 1, :]


def _attn_safe_kernel(qt_ref, k_ref, vt_ref, nw_ref, o_ref, qz_ref, m_ref, l_ref, acc_ref):
    ki = pl.program_id(2)

    @pl.when(ki == 0)
    def _init():
        _load_queries(qt_ref, qz_ref)
        m_ref[...] = jnp.full(m_ref.shape, -jnp.inf, F32)
        l_ref[...] = jnp.zeros(l_ref.shape, F32)
        acc_ref[...] = jnp.zeros(acc_ref.shape, F32)

    k = k_ref[...]
    for h in range(N_Q_HEADS):
        kv = h // Q_PER_KV
        s = _dot(k, qz_ref[h])
        m_old = m_ref[h]
        m_new = jnp.maximum(m_old, jnp.max(s, axis=0, keepdims=True))
        alpha = jnp.exp2(m_old - m_new)
        p = jnp.exp2(s - m_new)
        l_ref[h] = alpha * l_ref[h] + jnp.sum(p, axis=0, keepdims=True)
        vt = vt_ref[kv * VT_ROWS:kv * VT_ROWS + HEAD_DIM, :]
        acc_ref[h] = alpha * acc_ref[h] + _dot(vt, p.astype(BF16))
        m_ref[h] = m_new

    @pl.when(ki == pl.num_programs(2) - 1)
    def _finish():
        o_ref[...] = _normalise_heads(lambda h: acc_ref[h], lambda h: l_ref[h], nw_ref)


def _attention(qt, k, vt, out_norm_w, k_norm_w, batch, seq):
    nw = out_norm_w[:, None]
    params = pltpu.CompilerParams(dimension_semantics=("parallel", "parallel", "arbitrary"),
                                  vmem_limit_bytes=VMEM_LIMIT_BYTES)
    out_shape = jax.ShapeDtypeStruct((batch * seq, ATT_WIDTH), F32)

    def specs(tq, tk):
        nq, nk = seq // tq, seq // tk
        return ([pl.BlockSpec((None, ATT_WIDTH, tq), lambda b, qi, ki: (b, 0, qi)),
                 pl.BlockSpec((tk, KV_WIDTH), lambda b, qi, ki: (b * nk + ki, 0)),
                 pl.BlockSpec((None, N_KV_HEADS * VT_ROWS, tk), lambda b, qi, ki: (b, 0, ki)),
                 pl.BlockSpec(nw.shape, lambda b, qi, ki: (0, 0))],
                pl.BlockSpec((tq, ATT_WIDTH), lambda b, qi, ki: (b * nq + qi, 0)), (batch, nq, nk))

    kmax = (HEAD_DIM ** 0.5 * 1.02) * jnp.max(jnp.abs(k_norm_w)).reshape(1, 1)
    tq, tk = ATT_Q_ROWS, ATT_K_ROWS
    nq = seq // tq
    in_specs, out_spec, grid = specs(tq, tk)
    att, l = pl.pallas_call(
        _attn_kernel,
        grid=grid,
        in_specs=[pl.BlockSpec((1, 1), lambda b, qi, ki: (0, 0))] + in_specs,
        out_specs=[out_spec, pl.BlockSpec((None, N_Q_HEADS, tq), lambda b, qi, ki: (b * nq + qi, 0, 0))],
        out_shape=[out_shape, jax.ShapeDtypeStruct((batch * nq, N_Q_HEADS, tq), F32)],
        scratch_shapes=[pltpu.VMEM((N_Q_HEADS, KV_WIDTH, tq), BF16),
                        pltpu.VMEM((N_Q_HEADS, 1, tq), F32),
                        pltpu.VMEM((N_Q_HEADS, VT_ROWS, tq), F32)],
        compiler_params=params,
        name="attention",
    )(kmax, qt, k, vt, nw)

    def safe():
        rows = ATT_SAFE_ROWS
        in_specs, out_spec, grid = specs(rows, rows)
        return pl.pallas_call(
            _attn_safe_kernel,
            grid=grid,
            in_specs=in_specs,
            out_specs=out_spec,
            out_shape=out_shape,
            scratch_shapes=[pltpu.VMEM((N_Q_HEADS, KV_WIDTH, rows), BF16),
                            pltpu.VMEM((N_Q_HEADS, 1, rows), F32),
                            pltpu.VMEM((N_Q_HEADS, 1, rows), F32),
                            pltpu.VMEM((N_Q_HEADS, HEAD_DIM, rows), F32)],
            compiler_params=params,
            name="attention_safe",
        )(qt, k, vt, nw)

    return lax.cond(jnp.min(l) >= ATT_MIN_SUM, lambda: att, safe)


def _hgrn_block(qh, lf, kk, v, st_ref, reverse):
    n = HG_BLOCK
    row = lax.broadcasted_iota(jnp.int32, (n, n), 0)
    col = lax.broadcasted_iota(jnp.int32, (n, n), 1)
    tri = (col >= row) if reverse else (col <= row)
    tri_bf16 = jnp.where(tri, 1.0, 0.0).astype(BF16)
    first_head = col < HG_DIM
    same_head = (row < HG_DIM) == first_head

    lf_hi, lf_lo = _split_bf16(lf)
    b = _dot(tri_bf16, lf_hi) + _dot(tri_bf16, lf_lo)
    if reverse:
        b_end, b_mid = b[0:1, :], b[n // 2:n // 2 + 1, :]
    else:
        b_end, b_mid = b[n - 1:n, :], b[n // 2 - 1:n // 2, :]

    qf = qh.astype(F32)
    kf = kk.astype(F32)
    q_in = (qf * jnp.exp(b - b_mid)).astype(BF16)
    k_in = (kf * jnp.exp(b_mid - b)).astype(BF16)
    q_st = (qf * jnp.exp(b)).astype(BF16)
    k_st = (kf * jnp.exp(b_end - b)).astype(BF16)
    decay = jnp.exp(b_end)

    zero = jnp.zeros((), BF16)
    outs = []
    for p in range(HG_PAIRS):
        sl = slice(p * n, (p + 1) * n)
        q_p, k_p, v_p = q_in[:, sl], k_in[:, sl], v[:, sl]
        a0 = jnp.where(tri, _dot_nt(jnp.where(first_head, q_p, zero), k_p), 0.0).astype(BF16)
        a1 = jnp.where(tri, _dot_nt(jnp.where(first_head, zero, q_p), k_p), 0.0).astype(BF16)
        st = st_ref[p]
        o_p = (_dot(a0, jnp.where(first_head, v_p, zero)) + _dot(a1, jnp.where(first_head, zero, v_p))
               + _dot_nt(q_st[:, sl], st.astype(BF16)))
        outs.append(o_p)
        upd = _dot(v_p.astype(F32).T.astype(BF16), k_st[:, sl])
        st_ref[p] = st * decay[:, sl] + jnp.where(same_head, upd, 0.0)
    return jnp.concatenate(outs, axis=1)


def _hgrn_kernel(qf_ref, lff_ref, kf_ref, vf_ref, qb_ref, lfb_ref, kb_ref, vb_ref,
                 of_ref, ob_ref, stf_ref, stb_ref):
    @pl.when(pl.program_id(1) == 0)
    def _init():
        stf_ref[...] = jnp.zeros(stf_ref.shape, F32)
        stb_ref[...] = jnp.zeros(stb_ref.shape, F32)

    nblk = qf_ref.shape[0] // HG_BLOCK
    for j in range(nblk):
        rows = slice(j * HG_BLOCK, (j + 1) * HG_BLOCK)
        of_ref[rows, :] = _hgrn_block(qf_ref[rows, :], lff_ref[rows, :], kf_ref[rows, :],
                                      vf_ref[rows, :], stf_ref, reverse=False)
    for j in reversed(range(nblk)):
        rows = slice(j * HG_BLOCK, (j + 1) * HG_BLOCK)
        ob_ref[rows, :] = _hgrn_block(qb_ref[rows, :], lfb_ref[rows, :], kb_ref[rows, :],
                                      vb_ref[rows, :], stb_ref, reverse=True)


def _hgrn(hq, lff, kf, lfb, kb, hv, batch, seq):
    tc = HG_ROWS
    nt = seq // tc
    fwd = pl.BlockSpec((tc, HG_WIDTH), lambda b, n: (b * nt + n, 0))
    bwd = pl.BlockSpec((tc, HG_WIDTH), lambda b, n: (b * nt + nt - 1 - n, 0))
    out = jax.ShapeDtypeStruct((batch * seq, HG_WIDTH), F32)
    state = pltpu.VMEM((HG_PAIRS, HG_BLOCK, HG_BLOCK), F32)
    return pl.pallas_call(
        _hgrn_kernel,
        grid=(batch, nt),
        in_specs=[fwd, fwd, fwd, fwd, bwd, bwd, bwd, bwd],
        out_specs=[fwd, bwd],
        out_shape=[out, out],
        scratch_shapes=[state, state],
        compiler_params=pltpu.CompilerParams(dimension_semantics=("parallel", "arbitrary"),
                                             vmem_limit_bytes=VMEM_LIMIT_BYTES),
        name="hgrn",
    )(hq, lff, kf, hv, hq, lfb, kb, hv)


def _route(scores, biased):
    tm = scores.shape[1]
    neg_inf = -jnp.inf
    group_scores = []
    for g in range(N_GROUPS):
        blk = biased[g * EXPERTS_PER_GROUP:(g + 1) * EXPERTS_PER_GROUP, :]
        m1 = jnp.max(blk, axis=0, keepdims=True)
        n_max = jnp.sum(jnp.where(blk == m1, 1.0, 0.0), axis=0, keepdims=True)
        below = jnp.max(jnp.where(blk < m1, blk, neg_inf), axis=0, keepdims=True)
        group_scores.append(m1 + jnp.where(n_max >= 2.0, m1, below))
    gs = jnp.concatenate(group_scores, axis=0)

    gidx = lax.broadcasted_iota(jnp.int32, (N_GROUPS, tm), 0)
    rank = jnp.zeros((N_GROUPS, tm), F32)
    for g in range(N_GROUPS):
        other = gs[g:g + 1, :]
        ahead = jnp.where(other > gs, 1.0, jnp.where(other == gs, jnp.where(gidx > g, 1.0, 0.0), 0.0))
        rank = rank + ahead
    group_on = jnp.where(rank < float(TOPK_GROUPS), 1.0, 0.0)
    expert_on = jnp.concatenate(
        [jnp.broadcast_to(group_on[g:g + 1, :], (EXPERTS_PER_GROUP, tm)) for g in range(N_GROUPS)], axis=0)

    eidx = lax.broadcasted_iota(jnp.int32, (N_EXPERTS, tm), 0).astype(F32)
    work = jnp.where(expert_on > 0.0, biased, neg_inf)
    chosen = jnp.zeros((N_EXPERTS, tm), F32)
    for _ in range(TOP_K):
        best = jnp.max(work, axis=0, keepdims=True)
        first = jnp.min(jnp.where(work == best, eidx, float(N_EXPERTS)), axis=0, keepdims=True)
        pick = eidx == first
        chosen = jnp.where(pick, 1.0, chosen)
        work = jnp.where(pick, neg_inf, work)
    w = jnp.where(chosen > 0.0, scores, 0.0)
    return w / jnp.sum(w, axis=0, keepdims=True) * ROUTED_SCALE


def _mix_kernel(x_ref, att_ref, of_ref, ob_ref, g_ref, wout_ref, hnw_ref, lnw_ref, lnb_ref,
                wrhi_ref, wrlo_ref, rb_ref, ones_ref, x1_ref, gate_ref):
    o = of_ref[...] + ob_ref[...]
    o = o * lax.rsqrt(_group_mean_sq(o, ones_ref[...]) + RMS_EPS) * hnw_ref[...] * g_ref[...].astype(F32)
    mixed = jnp.concatenate([att_ref[...].astype(BF16), o.astype(BF16)], axis=1)
    h = DEEPNORM_ALPHA * x_ref[...] + _dot(mixed, wout_ref[...])
    x1 = _layer_norm(h, lnw_ref[...], lnb_ref[...])
    x1_ref[...] = x1

    x_hi, x_lo = _split_bf16(x1)
    w_hi, w_lo = wrhi_ref[...], wrlo_ref[...]
    logits = _dot_nt(w_hi, x_hi) + _dot_nt(w_hi, x_lo) + _dot_nt(w_lo, x_hi)
    scores = _sigmoid(logits)
    gate = _route(scores, scores + rb_ref[...])
    gate = jnp.concatenate([gate, jnp.zeros_like(gate)], axis=0)
    gate_ref[...] = gate.T


def _mix(x2, att, o_f, o_b, hg, w_out, hg_norm_w, ln_w, ln_b, w_router, router_bias, ones_blockdiag):
    t = x2.shape[0]
    tm = MIX_ROWS
    wr_hi, wr_lo = _split_bf16(w_router.T)
    args = (x2, att, o_f, o_b, hg, w_out, hg_norm_w[None, :], ln_w[None, :], ln_b[None, :],
            wr_hi, wr_lo, router_bias[:, None], ones_blockdiag)
    row_block = lambda w: pl.BlockSpec((tm, w), lambda i: (i, 0))
    full = lambda a: pl.BlockSpec(a.shape, lambda i: (0,) * a.ndim)
    return pl.pallas_call(
        _mix_kernel,
        grid=(t // tm,),
        in_specs=[row_block(D_MODEL), row_block(ATT_WIDTH), row_block(HG_WIDTH), row_block(HG_WIDTH),
                  row_block(HG_WIDTH)] + [full(a) for a in args[5:]],
        out_specs=[row_block(D_MODEL), row_block(LANES)],
        out_shape=[jax.ShapeDtypeStruct((t, D_MODEL), F32), jax.ShapeDtypeStruct((t, LANES), F32)],
        compiler_params=pltpu.CompilerParams(dimension_semantics=("parallel",),
                                             vmem_limit_bytes=VMEM_LIMIT_BYTES),
        name="mix",
    )(*args)


def _swiglu(xb, w_gate, w_up):
    hg = _dot(xb, w_gate)
    return hg * _sigmoid(hg) * _dot(xb, w_up)


def _moe_kernel(x1_ref, gate_ref, wg_ref, wu_ref, wd_ref, wsg_ref, wsu_ref, wsd_ref, lnw_ref, lnb_ref,
                o_ref, xb_ref, acc_ref):
    e = pl.program_id(1)

    @pl.when(e == 0)
    def _init():
        xb = x1_ref[...].astype(BF16)
        xb_ref[...] = xb
        acc_ref[...] = _dot(_swiglu(xb, wsg_ref[...], wsu_ref[...]).astype(BF16), wsd_ref[...])

    xb = xb_ref[...]
    hid = _swiglu(xb, wg_ref[...], wu_ref[...])
    lane = lax.broadcasted_iota(jnp.int32, gate_ref.shape, 1)
    g = jnp.sum(jnp.where(lane == e, gate_ref[...], 0.0), axis=1, keepdims=True)
    acc_ref[...] += _dot((hid * g).astype(BF16), wd_ref[...])

    @pl.when(e == pl.num_programs(1) - 1)
    def _finish():
        h = DEEPNORM_ALPHA * x1_ref[...] + acc_ref[...]
        o_ref[...] = _layer_norm(h, lnw_ref[...], lnb_ref[...])


def _moe(x1, gate, w_eg, w_eu, w_ed, w_sg, w_su, w_sd, ln_w, ln_b):
    t = x1.shape[0]
    tm = MOE_ROWS
    args = (x1, gate, w_eg, w_eu, w_ed, w_sg, w_su, w_sd, ln_w[None, :], ln_b[None, :])
    row_block = lambda w: pl.BlockSpec((tm, w), lambda i, e: (i, 0))
    full = lambda a: pl.BlockSpec(a.shape, lambda i, e: (0,) * a.ndim)
    expert = lambda a: pl.BlockSpec((None,) + a.shape[1:], lambda i, e: (e, 0, 0))
    return pl.pallas_call(
        _moe_kernel,
        grid=(t // tm, N_EXPERTS),
        in_specs=[row_block(D_MODEL), row_block(LANES), expert(w_eg), expert(w_eu), expert(w_ed)]
                 + [full(a) for a in args[5:]],
        out_specs=row_block(D_MODEL),
        out_shape=jax.ShapeDtypeStruct((t, D_MODEL), F32),
        scratch_shapes=[pltpu.VMEM((tm, D_MODEL), BF16), pltpu.VMEM((tm, D_MODEL), F32)],
        compiler_params=pltpu.CompilerParams(dimension_semantics=("parallel", "arbitrary"),
                                             vmem_limit_bytes=VMEM_LIMIT_BYTES),
        name="moe",
    )(*args)


def _trunk(x, p):
    batch, seq, _ = x.shape
    x2 = x.reshape(batch * seq, D_MODEL)
    qt, k, vt, hq, lff, kf, lfb, kb, hv, hg = _project(
        x2, batch, seq, p["w_in"], p["q_norm_w"], p["k_norm_w"], p["lb"], p["ones"])
    att = _attention(qt, k, vt, p["attn_out_norm_w"], p["k_norm_w"], batch, seq)
    o_f, o_b = _hgrn(hq, lff, kf, lfb, kb, hv, batch, seq)
    x1, gate = _mix(x2, att, o_f, o_b, hg, p["w_out"], p["hgrn_out_norm_w"], p["ln1_w"], p["ln1_b"],
                    p["w_router"], p["router_bias"], p["ones"])
    y = _moe(x1, gate, p["w_exp_gate"], p["w_exp_up"], p["w_exp_down"], p["w_sh_gate"], p["w_sh_up"],
             p["w_sh_down"], p["ln2_w"], p["ln2_b"])
    return y.reshape(batch, seq, D_MODEL)


def _prepare(w_in, q_norm_w, k_norm_w, attn_out_norm_w, hgrn_lb_logits, hgrn_out_norm_w, w_out, ln1_w, ln1_b, w_router, router_bias, w_exp_gate, w_exp_up, w_exp_down, w_sh_gate, w_sh_up, w_sh_down, ln2_w, ln2_b):
    layer = 0
    lb_all = jnp.cumsum(jax.nn.softmax(hgrn_lb_logits.astype(F32), axis=0), axis=0)
    head_of = jnp.arange(ATT_WIDTH) // HEAD_DIM
    return {
        "w_in": w_in[layer].astype(BF16),
        "q_norm_w": q_norm_w[layer], "k_norm_w": k_norm_w[layer],
        "attn_out_norm_w": attn_out_norm_w[layer],
        "lb": lb_all[layer],
        "hgrn_out_norm_w": hgrn_out_norm_w[layer],
        "w_out": w_out[layer].astype(BF16),
        "ln1_w": ln1_w[layer], "ln1_b": ln1_b[layer],
        "w_router": w_router[layer], "router_bias": router_bias[layer],
        "w_exp_gate": w_exp_gate[layer].astype(BF16), "w_exp_up": w_exp_up[layer].astype(BF16),
        "w_exp_down": w_exp_down[layer].astype(BF16),
        "w_sh_gate": w_sh_gate[layer].astype(BF16), "w_sh_up": w_sh_up[layer].astype(BF16),
        "w_sh_down": w_sh_down[layer].astype(BF16),
        "ln2_w": ln2_w[layer], "ln2_b": ln2_b[layer],
        "ones": (head_of[:, None] == head_of[None, :]).astype(BF16),
    }


def kernel(x_prompt, x_sample, w_in, q_norm_w, k_norm_w, attn_out_norm_w, hgrn_lb_logits, hgrn_out_norm_w, w_out, ln1_w, ln1_b, w_router, router_bias, w_exp_gate, w_exp_up, w_exp_down, w_sh_gate, w_sh_up, w_sh_down, ln2_w, ln2_b):
    p = _prepare(w_in, q_norm_w, k_norm_w, attn_out_norm_w, hgrn_lb_logits, hgrn_out_norm_w, w_out, ln1_w, ln1_b, w_router, router_bias, w_exp_gate, w_exp_up, w_exp_down, w_sh_gate, w_sh_up, w_sh_down, ln2_w, ln2_b)
    return _trunk(x_prompt, p), _trunk(x_sample, p)
```

```python
import functools

import jax
import jax.numpy as jnp
from jax import lax
from jax.experimental import pallas as pl
from jax.experimental.pallas import tpu as pltpu

F32 = jnp.float32
BF16 = jnp.bfloat16

D_MODEL = 1024
N_Q_HEADS = 8
N_KV_HEADS = 2
HEAD_DIM = 64
Q_PER_KV = N_Q_HEADS // N_KV_HEADS
ATT_WIDTH = N_Q_HEADS * HEAD_DIM
KV_WIDTH = N_KV_HEADS * HEAD_DIM
ROPE_THETA = 10000.0
ROPE_AXIS_DIM = HEAD_DIM // 2
ROPE_HALF = ROPE_AXIS_DIM // 2
GRID_W = 64
HG_HEADS = 8
HG_DIM = 64
HG_WIDTH = HG_HEADS * HG_DIM
HG_BLOCK = 128
HG_PAIRS = HG_WIDTH // HG_BLOCK
N_EXPERTS = 64
N_GROUPS = 8
EXPERTS_PER_GROUP = N_EXPERTS // N_GROUPS
TOPK_GROUPS = 4
TOP_K = 8
EXPERT_HIDDEN = 256
ROUTED_SCALE = 2.5
DEPTH = 1
DEEPNORM_ALPHA = (2.0 * DEPTH) ** 0.25
RMS_EPS = 1e-6
LN_EPS = 1e-5

_OFF_Q = 0
_OFF_K = _OFF_Q + ATT_WIDTH
_OFF_V = _OFF_K + KV_WIDTH
_OFF_HQ = _OFF_V + KV_WIDTH
_OFF_HFF = _OFF_HQ + HG_WIDTH
_OFF_HFB = _OFF_HFF + HG_WIDTH
_OFF_HI = _OFF_HFB + HG_WIDTH
_OFF_HG = _OFF_HI + HG_WIDTH
IN_WIDTH = _OFF_HG + HG_WIDTH

VMEM_LIMIT_BYTES = 56 * 1024 * 1024
LANES = 128

LOG2_E = 1.4426950408889634
Q_SCALE = HEAD_DIM ** -0.5 * LOG2_E
VT_ROWS = HEAD_DIM + 16
ATT_MIN_SUM = 2.0 ** -80

PROJ_ROWS = 512
ATT_Q_ROWS = 1024
ATT_K_ROWS = 1024
ATT_SUB_K = 256
ATT_SUB_Q = 1024
ATT_SAFE_ROWS = 512
HG_ROWS = 256
HG_MAX_SPAN = 80.0
MIX_ROWS = 512
MOE_ROWS = 1024
MOE_GROUP = 4


def _dot(a, b):
    return jnp.dot(a, b, preferred_element_type=F32)


def _dot_nt(a, b):
    return lax.dot_general(a, b, (((1,), (1,)), ((), ())), preferred_element_type=F32)


def _split_bf16(x):
    hi = x.astype(BF16)
    lo = (x - hi.astype(F32)).astype(BF16)
    return hi, lo


def _sigmoid(x):
    return 1.0 / (1.0 + jnp.exp(-x))


def _group_mean_sq(x, ones_blockdiag):
    hi, lo = _split_bf16(x * x)
    return (_dot(hi, ones_blockdiag) + _dot(lo, ones_blockdiag)) * (1.0 / HEAD_DIM)


def _layer_norm(h, w, b):
    mu = jnp.mean(h, axis=-1, keepdims=True)
    hc = h - mu
    var = jnp.mean(hc * hc, axis=-1, keepdims=True)
    return hc * lax.rsqrt(var + LN_EPS) * w + b


def _rope(x, cos, sin_lo, sin_hi):
    n = x.shape[1]
    reps = n // LANES
    cos = jnp.concatenate([cos] * reps, axis=1)
    sin_lo = jnp.concatenate([sin_lo] * reps, axis=1)
    sin_hi = jnp.concatenate([sin_hi] * reps, axis=1)
    return (x * cos + pltpu.roll(x, n - ROPE_HALF, 1) * sin_lo
            + pltpu.roll(x, ROPE_HALF, 1) * sin_hi)


def _proj_kernel(x_ref, w_ref, cos_ref, sinlo_ref, sinhi_ref, qw_ref, kw_ref, lb_ref, ones_ref,
                 qt_ref, k_ref, vt_ref, hq_ref, lff_ref, kf_ref, lfb_ref, kb_ref, hv_ref, hg_ref):
    xb = x_ref[...].astype(BF16)

    def seg(off, width):
        return _dot(xb, w_ref[:, off:off + width])

    cos, sin_lo, sin_hi = cos_ref[...], sinlo_ref[...], sinhi_ref[...]

    q = seg(_OFF_Q, ATT_WIDTH)
    q = q * lax.rsqrt(_group_mean_sq(q, ones_ref[...]) + RMS_EPS) * qw_ref[...]
    q = _rope(q, cos, sin_lo, sin_hi) * Q_SCALE
    qt_ref[...] = q.T.astype(BF16)

    k = seg(_OFF_K, KV_WIDTH)
    k = k * lax.rsqrt(_group_mean_sq(k, ones_ref[0:KV_WIDTH, 0:KV_WIDTH]) + RMS_EPS) * kw_ref[...]
    k_ref[...] = _rope(k, cos, sin_lo, sin_hi).astype(BF16)

    vt = seg(_OFF_V, KV_WIDTH).T
    ones_rows = jnp.ones((VT_ROWS - HEAD_DIM, vt.shape[1]), F32)
    vt_ref[...] = jnp.concatenate(
        [piece for kv in range(N_KV_HEADS) for piece in (vt[kv * HEAD_DIM:(kv + 1) * HEAD_DIM, :], ones_rows)],
        axis=0).astype(BF16)

    hq = seg(_OFF_HQ, HG_WIDTH)
    hq_ref[...] = (hq * _sigmoid(hq)).astype(BF16)

    for off, row, lf_ref, kk_ref in ((_OFF_HFF, 0, lff_ref, kf_ref), (_OFF_HFB, 1, lfb_ref, kb_ref)):
        z = seg(off, HG_WIDTH)
        lb = lb_ref[row:row + 1, :]
        lf_ref[...] = jnp.log(lb + (1.0 - lb) * _sigmoid(z))
        kk_ref[...] = ((1.0 - lb) * _sigmoid(-z)).astype(BF16)

    hv_ref[...] = seg(_OFF_HI, HG_WIDTH).astype(BF16)
    hg = seg(_OFF_HG, HG_WIDTH)
    hg_ref[...] = (hg * _sigmoid(hg)).astype(BF16)


def _rope_tables(seq):
    pos = jnp.arange(seq, dtype=jnp.int32)
    row_ids = (pos // GRID_W).astype(F32)
    col_ids = (pos % GRID_W).astype(F32)
    inv_freq = ROPE_THETA ** (-jnp.arange(0, ROPE_AXIS_DIM, 2, dtype=F32) / ROPE_AXIS_DIM)
    ang_r = row_ids[:, None] * inv_freq
    ang_c = col_ids[:, None] * inv_freq
    ang = jnp.concatenate([ang_r, ang_r, ang_c, ang_c], axis=-1)
    ang = jnp.concatenate([ang] * (LANES // HEAD_DIM), axis=-1)
    lower = (jnp.arange(LANES) % ROPE_AXIS_DIM) < ROPE_HALF
    sin = jnp.sin(ang)
    return jnp.cos(ang), jnp.where(lower, -sin, 0.0), jnp.where(lower, 0.0, sin)


def _project(x2, batch, seq, w_in, q_norm_w, k_norm_w, lb, ones_blockdiag):
    t = batch * seq
    tm = PROJ_ROWS
    nt = seq // tm
    cos, sin_lo, sin_hi = _rope_tables(seq)
    qw = jnp.tile(q_norm_w, N_Q_HEADS)[None, :]
    kw = jnp.tile(k_norm_w, N_KV_HEADS)[None, :]

    row_block = lambda w: pl.BlockSpec((tm, w), lambda i: (i, 0))
    pos_block = pl.BlockSpec((tm, LANES), lambda i: (i % nt, 0))
    full = lambda a: pl.BlockSpec(a.shape, lambda i: (0,) * a.ndim)
    t_block = lambda w: pl.BlockSpec((None, w, tm), lambda i: (i // nt, 0, i % nt))

    hg_bf16 = jax.ShapeDtypeStruct((t, HG_WIDTH), BF16)
    hg_f32 = jax.ShapeDtypeStruct((t, HG_WIDTH), F32)
    return pl.pallas_call(
        _proj_kernel,
        grid=(t // tm,),
        in_specs=[row_block(D_MODEL), full(w_in), pos_block, pos_block, pos_block,
                  full(qw), full(kw), full(lb), full(ones_blockdiag)],
        out_specs=[t_block(ATT_WIDTH), row_block(KV_WIDTH), t_block(N_KV_HEADS * VT_ROWS),
                   row_block(HG_WIDTH), row_block(HG_WIDTH), row_block(HG_WIDTH),
                   row_block(HG_WIDTH), row_block(HG_WIDTH), row_block(HG_WIDTH), row_block(HG_WIDTH)],
        out_shape=[jax.ShapeDtypeStruct((batch, ATT_WIDTH, seq), BF16),
                   jax.ShapeDtypeStruct((t, KV_WIDTH), BF16),
                   jax.ShapeDtypeStruct((batch, N_KV_HEADS * VT_ROWS, seq), BF16),
                   hg_bf16, hg_f32, hg_bf16, hg_f32, hg_bf16, hg_bf16, hg_bf16],
        compiler_params=pltpu.CompilerParams(dimension_semantics=("parallel",),
                                             vmem_limit_bytes=VMEM_LIMIT_BYTES),
        name="proj",
    )(x2, w_in, cos, sin_lo, sin_hi, qw, kw, lb, ones_blockdiag)


def _load_queries(qt_ref, qz_ref):
    tq = qt_ref.shape[1]
    zeros = jnp.zeros((HEAD_DIM, tq), BF16)
    for h in range(N_Q_HEADS):
        qh = qt_ref[h * HEAD_DIM:(h + 1) * HEAD_DIM, :]
        qz_ref[h] = jnp.concatenate([qh, zeros] if h < Q_PER_KV else [zeros, qh], axis=0)


def _normalise_heads(numer, denom, nw_ref):
    outs = []
    for h in range(N_Q_HEADS):
        o = numer(h) / denom(h)
        ms = jnp.mean(o * o, axis=0, keepdims=True)
        outs.append(o * lax.rsqrt(ms + RMS_EPS) * nw_ref[h * HEAD_DIM:(h + 1) * HEAD_DIM, :])
    return jnp.concatenate(outs, axis=0).T


def _attn_kernel(kmax_ref, qt_ref, k_ref, vt_ref, nw_ref, o_ref, l_ref, qz_ref, c_ref, acc_ref):
    ki = pl.program_id(2)
    tq = qt_ref.shape[1]
    tk = k_ref.shape[0]

    @pl.when(ki == 0)
    def _init():
        _load_queries(qt_ref, qz_ref)
        for h in range(N_Q_HEADS):
            qf = qt_ref[h * HEAD_DIM:(h + 1) * HEAD_DIM, :].astype(F32)
            c_ref[h] = jnp.sqrt(jnp.sum(qf * qf, axis=0, keepdims=True)) * kmax_ref[...]
        acc_ref[...] = jnp.zeros(acc_ref.shape, F32)

    def finish_tile(tile):
        s, h, keys, qsl = tile
        kv = h // Q_PER_KV
        p = jnp.exp2(s - c_ref[h, :, qsl]).astype(BF16)
        acc_ref[h, :, qsl] += _dot(vt_ref[kv * VT_ROWS:(kv + 1) * VT_ROWS, keys], p)

    pending = None
    for h in range(N_Q_HEADS):
        for js in range(tk // ATT_SUB_K):
            keys = slice(js * ATT_SUB_K, (js + 1) * ATT_SUB_K)
            for qs in range(tq // ATT_SUB_Q):
                qsl = slice(qs * ATT_SUB_Q, (qs + 1) * ATT_SUB_Q)
                s = _dot(k_ref[keys, :], qz_ref[h, :, qsl])
                if pending is not None:
                    finish_tile(pending)
                pending = (s, h, keys, qsl)
    finish_tile(pending)

    @pl.when(ki == pl.num_programs(2) - 1)
    def _finish():
        o_ref[...] = _normalise_heads(lambda h: acc_ref[h, 0:HEAD_DIM, :],
                                      lambda h: acc_ref[h, HEAD_DIM:HEAD_DIM + 1, :], nw_ref)
        for h in range(N_Q_HEADS):
            l_ref[h:h + 1, :] = acc_ref[h, HEAD_DIM:HEAD_DIM + 1, :]


def _attn_safe_kernel(qt_ref, k_ref, vt_ref, nw_ref, o_ref, qz_ref, m_ref, l_ref, acc_ref):
    ki = pl.program_id(2)

    @pl.when(ki == 0)
    def _init():
        _load_queries(qt_ref, qz_ref)
        m_ref[...] = jnp.full(m_ref.shape, -jnp.inf, F32)
        l_ref[...] = jnp.zeros(l_ref.shape, F32)
        acc_ref[...] = jnp.zeros(acc_ref.shape, F32)

    k = k_ref[...]
    for h in range(N_Q_HEADS):
        kv = h // Q_PER_KV
        s = _dot(k, qz_ref[h])
        m_old = m_ref[h]
        m_new = jnp.maximum(m_old, jnp.max(s, axis=0, keepdims=True))
        alpha = jnp.exp2(m_old - m_new)
        p = jnp.exp2(s - m_new)
        l_ref[h] = alpha * l_ref[h] + jnp.sum(p, axis=0, keepdims=True)
        vt = vt_ref[kv * VT_ROWS:kv * VT_ROWS + HEAD_DIM, :]
        acc_ref[h] = alpha * acc_ref[h] + _dot(vt, p.astype(BF16))
        m_ref[h] = m_new

    @pl.when(ki == pl.num_programs(2) - 1)
    def _finish():
        o_ref[...] = _normalise_heads(lambda h: acc_ref[h], lambda h: l_ref[h], nw_ref)


def _attention(qt, k, vt, out_norm_w, k_norm_w, batch, seq):
    nw = out_norm_w[:, None]
    params = pltpu.CompilerParams(dimension_semantics=("parallel", "parallel", "arbitrary"),
                                  vmem_limit_bytes=VMEM_LIMIT_BYTES)
    out_shape = jax.ShapeDtypeStruct((batch * seq, ATT_WIDTH), F32)

    def specs(tq, tk):
        nq, nk = seq // tq, seq // tk
        return ([pl.BlockSpec((None, ATT_WIDTH, tq), lambda b, qi, ki: (b, 0, qi)),
                 pl.BlockSpec((tk, KV_WIDTH), lambda b, qi, ki: (b * nk + ki, 0)),
                 pl.BlockSpec((None, N_KV_HEADS * VT_ROWS, tk), lambda b, qi, ki: (b, 0, ki)),
                 pl.BlockSpec(nw.shape, lambda b, qi, ki: (0, 0))],
                pl.BlockSpec((tq, ATT_WIDTH), lambda b, qi, ki: (b * nq + qi, 0)), (batch, nq, nk))

    kmax = (HEAD_DIM ** 0.5 * 1.02) * jnp.max(jnp.abs(k_norm_w)).reshape(1, 1)
    tq, tk = ATT_Q_ROWS, ATT_K_ROWS
    nq = seq // tq
    in_specs, out_spec, grid = specs(tq, tk)
    att, l = pl.pallas_call(
        _attn_kernel,
        grid=grid,
        in_specs=[pl.BlockSpec((1, 1), lambda b, qi, ki: (0, 0))] + in_specs,
        out_specs=[out_spec, pl.BlockSpec((None, N_Q_HEADS, tq), lambda b, qi, ki: (b * nq + qi, 0, 0))],
        out_shape=[out_shape, jax.ShapeDtypeStruct((batch * nq, N_Q_HEADS, tq), F32)],
        scratch_shapes=[pltpu.VMEM((N_Q_HEADS, KV_WIDTH, tq), BF16),
                        pltpu.VMEM((N_Q_HEADS, 1, tq), F32),
                        pltpu.VMEM((N_Q_HEADS, VT_ROWS, tq), F32)],
        compiler_params=params,
        name="attention",
    )(kmax, qt, k, vt, nw)

    def safe():
        rows = ATT_SAFE_ROWS
        in_specs, out_spec, grid = specs(rows, rows)
        return pl.pallas_call(
            _attn_safe_kernel,
            grid=grid,
            in_specs=in_specs,
            out_specs=out_spec,
            out_shape=out_shape,
            scratch_shapes=[pltpu.VMEM((N_Q_HEADS, KV_WIDTH, rows), BF16),
                            pltpu.VMEM((N_Q_HEADS, 1, rows), F32),
                            pltpu.VMEM((N_Q_HEADS, 1, rows), F32),
                            pltpu.VMEM((N_Q_HEADS, HEAD_DIM, rows), F32)],
            compiler_params=params,
            name="attention_safe",
        )(qt, k, vt, nw)

    return lax.cond(jnp.min(l) >= ATT_MIN_SUM, lambda: att, safe)


def _hgrn_direct_scores(qf, kf, vf, b, reverse, ones_ref, rows_ref):
    n = HG_BLOCK
    b_ref, k_ref, v_ref = rows_ref
    b_ref[...] = b
    k_ref[...] = kf
    v_ref[...] = vf
    row = lax.broadcasted_iota(jnp.int32, (n, HG_WIDTH), 0)
    ones = ones_ref[...]

    def body(j, out):
        live = (row <= j) if reverse else (row >= j)
        decay = jnp.exp(jnp.where(live, b - b_ref[pl.ds(j, 1), :], -jnp.inf))
        hi, lo = _split_bf16(qf * k_ref[pl.ds(j, 1), :] * decay)
        return out + (_dot(hi, ones) + _dot(lo, ones)) * v_ref[pl.ds(j, 1), :]

    return lax.fori_loop(0, n, body, jnp.zeros((n, HG_WIDTH), F32))


def _hgrn_block(qh, lf, kk, v, st_ref, reverse, direct=None):
    n = HG_BLOCK
    row = lax.broadcasted_iota(jnp.int32, (n, n), 0)
    col = lax.broadcasted_iota(jnp.int32, (n, n), 1)
    tri = (col >= row) if reverse else (col <= row)
    tri_bf16 = jnp.where(tri, 1.0, 0.0).astype(BF16)
    first_head = col < HG_DIM
    same_head = (row < HG_DIM) == first_head

    lf_hi, lf_lo = _split_bf16(lf)
    b = _dot(tri_bf16, lf_hi) + _dot(tri_bf16, lf_lo)
    if reverse:
        b_end, b_mid = b[0:1, :], b[n // 2:n // 2 + 1, :]
    else:
        b_end, b_mid = b[n - 1:n, :], b[n // 2 - 1:n // 2, :]

    qf = qh.astype(F32)
    kf = kk.astype(F32)
    q_st = (qf * jnp.exp(b)).astype(BF16)
    k_st = (kf * jnp.exp(b_end - b)).astype(BF16)
    decay = jnp.exp(b_end)
    if direct is None:
        q_in = (qf * jnp.exp(b - b_mid)).astype(BF16)
        k_in = (kf * jnp.exp(b_mid - b)).astype(BF16)
        span = jnp.abs(b - b_mid)
        span = jnp.max(jnp.max(span, axis=0, keepdims=True), axis=1, keepdims=True)
    else:
        within = _hgrn_direct_scores(qf, kf, v.astype(F32), b, reverse, *direct)
        span = jnp.zeros((1, 1), F32)

    zero = jnp.zeros((), BF16)
    outs = []
    for p in range(HG_PAIRS):
        sl = slice(p * n, (p + 1) * n)
        v_p = v[:, sl]
        st = st_ref[p]
        o_p = _dot_nt(q_st[:, sl], st.astype(BF16))
        if direct is None:
            q_p, k_p = q_in[:, sl], k_in[:, sl]
            a0 = jnp.where(tri, _dot_nt(jnp.where(first_head, q_p, zero), k_p), 0.0).astype(BF16)
            a1 = jnp.where(tri, _dot_nt(jnp.where(first_head, zero, q_p), k_p), 0.0).astype(BF16)
            o_p = o_p + _dot(a0, jnp.where(first_head, v_p, zero)) + _dot(a1, jnp.where(first_head, zero, v_p))
        else:
            o_p = o_p + within[:, sl]
        outs.append(o_p)
        upd = _dot(v_p.astype(F32).T.astype(BF16), k_st[:, sl])
        st_ref[p] = st * decay[:, sl] + jnp.where(same_head, upd, 0.0)
    return jnp.concatenate(outs, axis=1), span


def _hgrn_scan(refs, of_ref, ob_ref, stf_ref, stb_ref, direct):
    qf_ref, lff_ref, kf_ref, vf_ref, qb_ref, lfb_ref, kb_ref, vb_ref = refs

    @pl.when(pl.program_id(1) == 0)
    def _init():
        stf_ref[...] = jnp.zeros(stf_ref.shape, F32)
        stb_ref[...] = jnp.zeros(stb_ref.shape, F32)

    nblk = qf_ref.shape[0] // HG_BLOCK
    spans = []
    for j in range(nblk):
        rows = slice(j * HG_BLOCK, (j + 1) * HG_BLOCK)
        of_ref[rows, :], span = _hgrn_block(qf_ref[rows, :], lff_ref[rows, :], kf_ref[rows, :],
                                            vf_ref[rows, :], stf_ref, False, direct)
        spans.append(span)
    for j in reversed(range(nblk)):
        rows = slice(j * HG_BLOCK, (j + 1) * HG_BLOCK)
        ob_ref[rows, :], span = _hgrn_block(qb_ref[rows, :], lfb_ref[rows, :], kb_ref[rows, :],
                                            vb_ref[rows, :], stb_ref, True, direct)
        spans.append(span)
    return functools.reduce(jnp.maximum, spans)


def _hgrn_kernel(*refs):
    *ins, of_ref, ob_ref, span_ref, stf_ref, stb_ref = refs
    span = _hgrn_scan(ins, of_ref, ob_ref, stf_ref, stb_ref, None)
    span_ref[...] = jnp.broadcast_to(span, span_ref.shape)


def _hgrn_safe_kernel(*refs):
    ones_ref, *ins, of_ref, ob_ref, stf_ref, stb_ref, b_ref, k_ref, v_ref = refs
    _hgrn_scan(ins, of_ref, ob_ref, stf_ref, stb_ref, (ones_ref, (b_ref, k_ref, v_ref)))


def _hgrn(hq, lff, kf, lfb, kb, hv, ones_blockdiag, batch, seq):
    tc = HG_ROWS
    nt = seq // tc
    fwd = pl.BlockSpec((tc, HG_WIDTH), lambda b, n: (b * nt + n, 0))
    bwd = pl.BlockSpec((tc, HG_WIDTH), lambda b, n: (b * nt + nt - 1 - n, 0))
    out = jax.ShapeDtypeStruct((batch * seq, HG_WIDTH), F32)
    state = pltpu.VMEM((HG_PAIRS, HG_BLOCK, HG_BLOCK), F32)
    params = pltpu.CompilerParams(dimension_semantics=("parallel", "arbitrary"),
                                  vmem_limit_bytes=VMEM_LIMIT_BYTES)
    operands = (hq, lff, kf, hv, hq, lfb, kb, hv)
    o_f, o_b, span = pl.pallas_call(
        _hgrn_kernel,
        grid=(batch, nt),
        in_specs=[fwd, fwd, fwd, fwd, bwd, bwd, bwd, bwd],
        out_specs=[fwd, bwd, pl.BlockSpec((None, 8, LANES), lambda b, n: (b * nt + n, 0, 0))],
        out_shape=[out, out, jax.ShapeDtypeStruct((batch * nt, 8, LANES), F32)],
        scratch_shapes=[state, state],
        compiler_params=params,
        name="hgrn",
    )(*operands)

    def safe():
        rows = pltpu.VMEM((HG_BLOCK, HG_WIDTH), F32)
        return tuple(pl.pallas_call(
            _hgrn_safe_kernel,
            grid=(batch, nt),
            in_specs=[pl.BlockSpec(ones_blockdiag.shape, lambda b, n: (0, 0)),
                      fwd, fwd, fwd, fwd, bwd, bwd, bwd, bwd],
            out_specs=[fwd, bwd],
            out_shape=[out, out],
            scratch_shapes=[state, state, rows, rows, rows],
            compiler_params=params,
            name="hgrn_safe",
        )(ones_blockdiag, *operands))

    return lax.cond(jnp.max(span) < HG_MAX_SPAN, lambda: (o_f, o_b), safe)


def _route(scores, biased):
    tm = scores.shape[1]
    neg_inf = -jnp.inf
    group_scores = []
    for g in range(N_GROUPS):
        blk = biased[g * EXPERTS_PER_GROUP:(g + 1) * EXPERTS_PER_GROUP, :]
        m1 = jnp.max(blk, axis=0, keepdims=True)
        n_max = jnp.sum(jnp.where(blk == m1, 1.0, 0.0), axis=0, keepdims=True)
        below = jnp.max(jnp.where(blk < m1, blk, neg_inf), axis=0, keepdims=True)
        group_scores.append(m1 + jnp.where(n_max >= 2.0, m1, below))
    gs = jnp.concatenate(group_scores, axis=0)

    gidx = lax.broadcasted_iota(jnp.int32, (N_GROUPS, tm), 0)
    rank = jnp.zeros((N_GROUPS, tm), F32)
    for g in range(N_GROUPS):
        other = gs[g:g + 1, :]
        ahead = jnp.where(other > gs, 1.0, jnp.where(other == gs, jnp.where(gidx > g, 1.0, 0.0), 0.0))
        rank = rank + ahead
    group_on = jnp.where(rank < float(TOPK_GROUPS), 1.0, 0.0)
    expert_on = jnp.concatenate(
        [jnp.broadcast_to(group_on[g:g + 1, :], (EXPERTS_PER_GROUP, tm)) for g in range(N_GROUPS)], axis=0)

    eidx = lax.broadcasted_iota(jnp.int32, (N_EXPERTS, tm), 0).astype(F32)
    work = jnp.where(expert_on > 0.0, biased, neg_inf)
    chosen = jnp.zeros((N_EXPERTS, tm), F32)
    for _ in range(TOP_K):
        best = jnp.max(work, axis=0, keepdims=True)
        first = jnp.min(jnp.where(work == best, eidx, float(N_EXPERTS)), axis=0, keepdims=True)
        pick = eidx == first
        chosen = jnp.where(pick, 1.0, chosen)
        work = jnp.where(pick, neg_inf, work)
    w = jnp.where(chosen > 0.0, scores, 0.0)
    return w / jnp.sum(w, axis=0, keepdims=True) * ROUTED_SCALE


def _mix_kernel(x_ref, att_ref, of_ref, ob_ref, g_ref, wout_ref, hnw_ref, lnw_ref, lnb_ref,
                wrhi_ref, wrlo_ref, rb_ref, ones_ref, x1_ref, gate_ref):
    o = of_ref[...] + ob_ref[...]
    o = o * lax.rsqrt(_group_mean_sq(o, ones_ref[...]) + RMS_EPS) * hnw_ref[...] * g_ref[...].astype(F32)
    mixed = jnp.concatenate([att_ref[...].astype(BF16), o.astype(BF16)], axis=1)
    h = DEEPNORM_ALPHA * x_ref[...] + _dot(mixed, wout_ref[...])
    x1 = _layer_norm(h, lnw_ref[...], lnb_ref[...])
    x1_ref[...] = x1

    x_hi, x_lo = _split_bf16(x1)
    w_hi, w_lo = wrhi_ref[...], wrlo_ref[...]
    logits = _dot_nt(w_hi, x_hi) + _dot_nt(w_hi, x_lo) + _dot_nt(w_lo, x_hi)
    scores = _sigmoid(logits)
    gate = _route(scores, scores + rb_ref[...])
    gate_ref[...] = jnp.concatenate([gate, jnp.zeros_like(gate)], axis=0).T


def _mix(x2, att, o_f, o_b, hg, w_out, hg_norm_w, ln_w, ln_b, w_router, router_bias, ones_blockdiag):
    t = x2.shape[0]
    tm = MIX_ROWS
    wr_hi, wr_lo = _split_bf16(w_router.T)
    args = (x2, att, o_f, o_b, hg, w_out, hg_norm_w[None, :], ln_w[None, :], ln_b[None, :],
            wr_hi, wr_lo, router_bias[:, None], ones_blockdiag)
    row_block = lambda w: pl.BlockSpec((tm, w), lambda i: (i, 0))
    full = lambda a: pl.BlockSpec(a.shape, lambda i: (0,) * a.ndim)
    return pl.pallas_call(
        _mix_kernel,
        grid=(t // tm,),
        in_specs=[row_block(D_MODEL), row_block(ATT_WIDTH), row_block(HG_WIDTH), row_block(HG_WIDTH),
                  row_block(HG_WIDTH)] + [full(a) for a in args[5:]],
        out_specs=[row_block(D_MODEL), row_block(LANES)],
        out_shape=[jax.ShapeDtypeStruct((t, D_MODEL), F32), jax.ShapeDtypeStruct((t, LANES), F32)],
        compiler_params=pltpu.CompilerParams(dimension_semantics=("parallel",),
                                             vmem_limit_bytes=VMEM_LIMIT_BYTES),
        name="mix",
    )(*args)


def _swiglu(xb, w_gate, w_up):
    hg = _dot(xb, w_gate)
    return hg * _sigmoid(hg) * _dot(xb, w_up)


def _moe_kernel(x1_ref, gate_ref, wg_ref, wu_ref, wd_ref, wsg_ref, wsu_ref, wsd_ref, lnw_ref, lnb_ref,
                o_ref, xb_ref, acc_ref):
    step = pl.program_id(1)

    @pl.when(step == 0)
    def _init():
        xb = x1_ref[...].astype(BF16)
        xb_ref[...] = xb
        acc_ref[...] = _dot(_swiglu(xb, wsg_ref[...], wsu_ref[...]).astype(BF16), wsd_ref[...])

    xb = xb_ref[...]
    lane = lax.broadcasted_iota(jnp.int32, gate_ref.shape, 1)
    total = None
    for j in range(MOE_GROUP):
        hid = _swiglu(xb, wg_ref[j], wu_ref[j])
        g = jnp.sum(jnp.where(lane == step * MOE_GROUP + j, gate_ref[...], 0.0), axis=1, keepdims=True)
        part = _dot((hid * g).astype(BF16), wd_ref[j])
        total = part if total is None else total + part
    acc_ref[...] += total

    @pl.when(step == pl.num_programs(1) - 1)
    def _finish():
        h = DEEPNORM_ALPHA * x1_ref[...] + acc_ref[...]
        o_ref[...] = _layer_norm(h, lnw_ref[...], lnb_ref[...])


def _moe(x1, gate, w_eg, w_eu, w_ed, w_sg, w_su, w_sd, ln_w, ln_b):
    t = x1.shape[0]
    tm = MOE_ROWS
    args = (x1, gate, w_eg, w_eu, w_ed, w_sg, w_su, w_sd, ln_w[None, :], ln_b[None, :])
    row_block = lambda w: pl.BlockSpec((tm, w), lambda i, g: (i, 0))
    full = lambda a: pl.BlockSpec(a.shape, lambda i, g: (0,) * a.ndim)
    experts = lambda a: pl.BlockSpec((MOE_GROUP,) + a.shape[1:], lambda i, g: (g, 0, 0))
    return pl.pallas_call(
        _moe_kernel,
        grid=(t // tm, N_EXPERTS // MOE_GROUP),
        in_specs=[row_block(D_MODEL), row_block(LANES), experts(w_eg), experts(w_eu), experts(w_ed)]
                 + [full(a) for a in args[5:]],
        out_specs=row_block(D_MODEL),
        out_shape=jax.ShapeDtypeStruct((t, D_MODEL), F32),
        scratch_shapes=[pltpu.VMEM((tm, D_MODEL), BF16), pltpu.VMEM((tm, D_MODEL), F32)],
        compiler_params=pltpu.CompilerParams(dimension_semantics=("parallel", "arbitrary"),
                                             vmem_limit_bytes=VMEM_LIMIT_BYTES),
        name="moe",
    )(*args)


def _trunk(x, p):
    batch, seq, _ = x.shape
    x2 = x.reshape(batch * seq, D_MODEL)
    qt, k, vt, hq, lff, kf, lfb, kb, hv, hg = _project(
        x2, batch, seq, p["w_in"], p["q_norm_w"], p["k_norm_w"], p["lb"], p["ones"])
    att = _attention(qt, k, vt, p["attn_out_norm_w"], p["k_norm_w"], batch, seq)
    o_f, o_b = _hgrn(hq, lff, kf, lfb, kb, hv, p["ones"], batch, seq)
    x1, gate = _mix(x2, att, o_f, o_b, hg, p["w_out"], p["hgrn_out_norm_w"], p["ln1_w"], p["ln1_b"],
                    p["w_router"], p["router_bias"], p["ones"])
    y = _moe(x1, gate, p["w_exp_gate"], p["w_exp_up"], p["w_exp_down"], p["w_sh_gate"], p["w_sh_up"],
             p["w_sh_down"], p["ln2_w"], p["ln2_b"])
    return y.reshape(batch, seq, D_MODEL)


def _prepare(w_in, q_norm_w, k_norm_w, attn_out_norm_w, hgrn_lb_logits, hgrn_out_norm_w, w_out, ln1_w, ln1_b, w_router, router_bias, w_exp_gate, w_exp_up, w_exp_down, w_sh_gate, w_sh_up, w_sh_down, ln2_w, ln2_b):
    layer = 0
    lb_all = jnp.cumsum(jax.nn.softmax(hgrn_lb_logits.astype(F32), axis=0), axis=0)
    head_of = jnp.arange(ATT_WIDTH) // HEAD_DIM
    return {
        "w_in": w_in[layer].astype(BF16),
        "q_norm_w": q_norm_w[layer], "k_norm_w": k_norm_w[layer],
        "attn_out_norm_w": attn_out_norm_w[layer],
        "lb": lb_all[layer],
        "hgrn_out_norm_w": hgrn_out_norm_w[layer],
        "w_out": w_out[layer].astype(BF16),
        "ln1_w": ln1_w[layer], "ln1_b": ln1_b[layer],
        "w_router": w_router[layer], "router_bias": router_bias[layer],
        "w_exp_gate": w_exp_gate[layer].astype(BF16), "w_exp_up": w_exp_up[layer].astype(BF16),
        "w_exp_down": w_exp_down[layer].astype(BF16),
        "w_sh_gate": w_sh_gate[layer].astype(BF16), "w_sh_up": w_sh_up[layer].astype(BF16),
        "w_sh_down": w_sh_down[layer].astype(BF16),
        "ln2_w": ln2_w[layer], "ln2_b": ln2_b[layer],
        "ones": (head_of[:, None] == head_of[None, :]).astype(BF16),
    }


def kernel(x_prompt, x_sample, w_in, q_norm_w, k_norm_w, attn_out_norm_w, hgrn_lb_logits, hgrn_out_norm_w, w_out, ln1_w, ln1_b, w_router, router_bias, w_exp_gate, w_exp_up, w_exp_down, w_sh_gate, w_sh_up, w_sh_down, ln2_w, ln2_b):
    p = _prepare(w_in, q_norm_w, k_norm_w, attn_out_norm_w, hgrn_lb_logits, hgrn_out_norm_w, w_out, ln1_w, ln1_b, w_router, router_bias, w_exp_gate, w_exp_up, w_exp_down, w_sh_gate, w_sh_up, w_sh_down, ln2_w, ln2_b)
    return _trunk(x_prompt, p), _trunk(x_sample, p)
```

```python
import functools

import jax
import jax.numpy as jnp
from jax import lax
from jax.experimental import pallas as pl
from jax.experimental.pallas import tpu as pltpu

F32 = jnp.float32
BF16 = jnp.bfloat16
FP8 = jnp.float8_e4m3fn
FP8_PEAK = 256.0
FP8_TINY = 1e-30

D_MODEL = 1024
N_Q_HEADS = 8
N_KV_HEADS = 2
HEAD_DIM = 64
Q_PER_KV = N_Q_HEADS // N_KV_HEADS
ATT_WIDTH = N_Q_HEADS * HEAD_DIM
KV_WIDTH = N_KV_HEADS * HEAD_DIM
ROPE_THETA = 10000.0
ROPE_AXIS_DIM = HEAD_DIM // 2
ROPE_HALF = ROPE_AXIS_DIM // 2
GRID_W = 64
HG_HEADS = 8
HG_DIM = 64
HG_WIDTH = HG_HEADS * HG_DIM
HG_BLOCK = 128
HG_PAIRS = HG_WIDTH // HG_BLOCK
N_EXPERTS = 64
N_GROUPS = 8
EXPERTS_PER_GROUP = N_EXPERTS // N_GROUPS
TOPK_GROUPS = 4
TOP_K = 8
EXPERT_HIDDEN = 256
ROUTED_SCALE = 2.5
DEPTH = 1
DEEPNORM_ALPHA = (2.0 * DEPTH) ** 0.25
RMS_EPS = 1e-6
LN_EPS = 1e-5

_OFF_Q = 0
_OFF_K = _OFF_Q + ATT_WIDTH
_OFF_V = _OFF_K + KV_WIDTH
_OFF_HQ = _OFF_V + KV_WIDTH
_OFF_HFF = _OFF_HQ + HG_WIDTH
_OFF_HFB = _OFF_HFF + HG_WIDTH
_OFF_HI = _OFF_HFB + HG_WIDTH
_OFF_HG = _OFF_HI + HG_WIDTH
IN_WIDTH = _OFF_HG + HG_WIDTH

VMEM_LIMIT_BYTES = 56 * 1024 * 1024
LANES = 128

LOG2_E = 1.4426950408889634
Q_SCALE = HEAD_DIM ** -0.5 * LOG2_E
VT_ROWS = HEAD_DIM + 16
ATT_MIN_SUM = 2.0 ** -80

PROJ_ROWS = 512
ATT_Q_ROWS = 1024
ATT_K_ROWS = 1024
ATT_SUB_K = 256
ATT_SUB_Q = 1024
ATT_SAFE_ROWS = 512
HG_ROWS = 256
HG_MAX_SPAN = 80.0
MIX_ROWS = 512
MOE_ROWS = 1024
MOE_GROUP = 4


def _dot(a, b):
    return jnp.dot(a, b, preferred_element_type=F32)


def _dot_nt(a, b):
    return lax.dot_general(a, b, (((1,), (1,)), ((), ())), preferred_element_type=F32)


def _split_bf16(x):
    hi = x.astype(BF16)
    lo = (x - hi.astype(F32)).astype(BF16)
    return hi, lo


def _sigmoid(x):
    return 1.0 / (1.0 + jnp.exp(-x))


def _group_mean_sq(x, ones_blockdiag):
    hi, lo = _split_bf16(x * x)
    return (_dot(hi, ones_blockdiag) + _dot(lo, ones_blockdiag)) * (1.0 / HEAD_DIM)


def _layer_norm(h, w, b):
    mu = jnp.mean(h, axis=-1, keepdims=True)
    hc = h - mu
    var = jnp.mean(hc * hc, axis=-1, keepdims=True)
    return hc * lax.rsqrt(var + LN_EPS) * w + b


def _rope(x, cos, sin_lo, sin_hi):
    n = x.shape[1]
    reps = n // LANES
    cos = jnp.concatenate([cos] * reps, axis=1)
    sin_lo = jnp.concatenate([sin_lo] * reps, axis=1)
    sin_hi = jnp.concatenate([sin_hi] * reps, axis=1)
    return (x * cos + pltpu.roll(x, n - ROPE_HALF, 1) * sin_lo
            + pltpu.roll(x, ROPE_HALF, 1) * sin_hi)


def _proj_kernel(x_ref, w_ref, cos_ref, sinlo_ref, sinhi_ref, qw_ref, kw_ref, lb_ref, ones_ref,
                 qt_ref, k_ref, vt_ref, hq_ref, lff_ref, kf_ref, lfb_ref, kb_ref, hv_ref, hg_ref):
    xb = x_ref[...].astype(BF16)

    def seg(off, width):
        return _dot(xb, w_ref[:, off:off + width])

    cos, sin_lo, sin_hi = cos_ref[...], sinlo_ref[...], sinhi_ref[...]

    q = seg(_OFF_Q, ATT_WIDTH)
    q = q * lax.rsqrt(_group_mean_sq(q, ones_ref[...]) + RMS_EPS) * qw_ref[...]
    q = _rope(q, cos, sin_lo, sin_hi) * Q_SCALE
    qt_ref[...] = q.T.astype(BF16)

    k = seg(_OFF_K, KV_WIDTH)
    k = k * lax.rsqrt(_group_mean_sq(k, ones_ref[0:KV_WIDTH, 0:KV_WIDTH]) + RMS_EPS) * kw_ref[...]
    k_ref[...] = _rope(k, cos, sin_lo, sin_hi).astype(BF16)

    vt = seg(_OFF_V, KV_WIDTH).T
    ones_rows = jnp.ones((VT_ROWS - HEAD_DIM, vt.shape[1]), F32)
    vt_ref[...] = jnp.concatenate(
        [piece for kv in range(N_KV_HEADS) for piece in (vt[kv * HEAD_DIM:(kv + 1) * HEAD_DIM, :], ones_rows)],
        axis=0).astype(BF16)

    hq = seg(_OFF_HQ, HG_WIDTH)
    hq_ref[...] = (hq * _sigmoid(hq)).astype(BF16)

    for off, row, lf_ref, kk_ref in ((_OFF_HFF, 0, lff_ref, kf_ref), (_OFF_HFB, 1, lfb_ref, kb_ref)):
        z = seg(off, HG_WIDTH)
        lb = lb_ref[row:row + 1, :]
        lf_ref[...] = jnp.log(lb + (1.0 - lb) * _sigmoid(z))
        kk_ref[...] = ((1.0 - lb) * _sigmoid(-z)).astype(BF16)

    hv_ref[...] = seg(_OFF_HI, HG_WIDTH).astype(BF16)
    hg = seg(_OFF_HG, HG_WIDTH)
    hg_ref[...] = (hg * _sigmoid(hg)).astype(BF16)


def _rope_tables(seq):
    pos = jnp.arange(seq, dtype=jnp.int32)
    row_ids = (pos // GRID_W).astype(F32)
    col_ids = (pos % GRID_W).astype(F32)
    inv_freq = ROPE_THETA ** (-jnp.arange(0, ROPE_AXIS_DIM, 2, dtype=F32) / ROPE_AXIS_DIM)
    ang_r = row_ids[:, None] * inv_freq
    ang_c = col_ids[:, None] * inv_freq
    ang = jnp.concatenate([ang_r, ang_r, ang_c, ang_c], axis=-1)
    ang = jnp.concatenate([ang] * (LANES // HEAD_DIM), axis=-1)
    lower = (jnp.arange(LANES) % ROPE_AXIS_DIM) < ROPE_HALF
    sin = jnp.sin(ang)
    return jnp.cos(ang), jnp.where(lower, -sin, 0.0), jnp.where(lower, 0.0, sin)


def _project(x2, batch, seq, w_in, q_norm_w, k_norm_w, lb, ones_blockdiag):
    t = batch * seq
    tm = PROJ_ROWS
    nt = seq // tm
    cos, sin_lo, sin_hi = _rope_tables(seq)
    qw = jnp.tile(q_norm_w, N_Q_HEADS)[None, :]
    kw = jnp.tile(k_norm_w, N_KV_HEADS)[None, :]

    row_block = lambda w: pl.BlockSpec((tm, w), lambda i: (i, 0))
    pos_block = pl.BlockSpec((tm, LANES), lambda i: (i % nt, 0))
    full = lambda a: pl.BlockSpec(a.shape, lambda i: (0,) * a.ndim)
    t_block = lambda w: pl.BlockSpec((None, w, tm), lambda i: (i // nt, 0, i % nt))

    hg_bf16 = jax.ShapeDtypeStruct((t, HG_WIDTH), BF16)
    hg_f32 = jax.ShapeDtypeStruct((t, HG_WIDTH), F32)
    return pl.pallas_call(
        _proj_kernel,
        grid=(t // tm,),
        in_specs=[row_block(D_MODEL), full(w_in), pos_block, pos_block, pos_block,
                  full(qw), full(kw), full(lb), full(ones_blockdiag)],
        out_specs=[t_block(ATT_WIDTH), row_block(KV_WIDTH), t_block(N_KV_HEADS * VT_ROWS),
                   row_block(HG_WIDTH), row_block(HG_WIDTH), row_block(HG_WIDTH),
                   row_block(HG_WIDTH), row_block(HG_WIDTH), row_block(HG_WIDTH), row_block(HG_WIDTH)],
        out_shape=[jax.ShapeDtypeStruct((batch, ATT_WIDTH, seq), BF16),
                   jax.ShapeDtypeStruct((t, KV_WIDTH), BF16),
                   jax.ShapeDtypeStruct((batch, N_KV_HEADS * VT_ROWS, seq), BF16),
                   hg_bf16, hg_f32, hg_bf16, hg_f32, hg_bf16, hg_bf16, hg_bf16],
        compiler_params=pltpu.CompilerParams(dimension_semantics=("parallel",),
                                             vmem_limit_bytes=VMEM_LIMIT_BYTES),
        name="proj",
    )(x2, w_in, cos, sin_lo, sin_hi, qw, kw, lb, ones_blockdiag)


def _load_queries(qt_ref, qz_ref):
    tq = qt_ref.shape[1]
    zeros = jnp.zeros((HEAD_DIM, tq), BF16)
    for h in range(N_Q_HEADS):
        qh = qt_ref[h * HEAD_DIM:(h + 1) * HEAD_DIM, :]
        qz_ref[h] = jnp.concatenate([qh, zeros] if h < Q_PER_KV else [zeros, qh], axis=0)


def _normalise_heads(numer, denom, nw_ref):
    outs = []
    for h in range(N_Q_HEADS):
        o = numer(h) / denom(h)
        ms = jnp.mean(o * o, axis=0, keepdims=True)
        outs.append(o * lax.rsqrt(ms + RMS_EPS) * nw_ref[h * HEAD_DIM:(h + 1) * HEAD_DIM, :])
    return jnp.concatenate(outs, axis=0).T


def _attn_kernel(kmax_ref, qt_ref, k_ref, vt_ref, nw_ref, o_ref, l_ref, qz_ref, c_ref, acc_ref):
    ki = pl.program_id(2)
    tq = qt_ref.shape[1]
    tk = k_ref.shape[0]

    @pl.when(ki == 0)
    def _init():
        _load_queries(qt_ref, qz_ref)
        for h in range(N_Q_HEADS):
            qf = qt_ref[h * HEAD_DIM:(h + 1) * HEAD_DIM, :].astype(F32)
            c_ref[h] = jnp.sqrt(jnp.sum(qf * qf, axis=0, keepdims=True)) * kmax_ref[...]
        acc_ref[...] = jnp.zeros(acc_ref.shape, F32)

    def finish_tile(tile):
        s, h, keys, qsl = tile
        kv = h // Q_PER_KV
        p = jnp.exp2(s - c_ref[h, :, qsl]).astype(BF16)
        acc_ref[h, :, qsl] += _dot(vt_ref[kv * VT_ROWS:(kv + 1) * VT_ROWS, keys], p)

    pending = None
    for h in range(N_Q_HEADS):
        for js in range(tk // ATT_SUB_K):
            keys = slice(js * ATT_SUB_K, (js + 1) * ATT_SUB_K)
            for qs in range(tq // ATT_SUB_Q):
                qsl = slice(qs * ATT_SUB_Q, (qs + 1) * ATT_SUB_Q)
                s = _dot(k_ref[keys, :], qz_ref[h, :, qsl])
                if pending is not None:
                    finish_tile(pending)
                pending = (s, h, keys, qsl)
    finish_tile(pending)

    @pl.when(ki == pl.num_programs(2) - 1)
    def _finish():
        o_ref[...] = _normalise_heads(lambda h: acc_ref[h, 0:HEAD_DIM, :],
                                      lambda h: acc_ref[h, HEAD_DIM:HEAD_DIM + 1, :], nw_ref)
        for h in range(N_Q_HEADS):
            l_ref[h:h + 1, :] = acc_ref[h, HEAD_DIM:HEAD_DIM + 1, :]


def _attn_safe_kernel(qt_ref, k_ref, vt_ref, nw_ref, o_ref, qz_ref, m_ref, l_ref, acc_ref):
    ki = pl.program_id(2)

    @pl.when(ki == 0)
    def _init():
        _load_queries(qt_ref, qz_ref)
        m_ref[...] = jnp.full(m_ref.shape, -jnp.inf, F32)
        l_ref[...] = jnp.zeros(l_ref.shape, F32)
        acc_ref[...] = jnp.zeros(acc_ref.shape, F32)

    k = k_ref[...]
    for h in range(N_Q_HEADS):
        kv = h // Q_PER_KV
        s = _dot(k, qz_ref[h])
        m_old = m_ref[h]
        m_new = jnp.maximum(m_old, jnp.max(s, axis=0, keepdims=True))
        alpha = jnp.exp2(m_old - m_new)
        p = jnp.exp2(s - m_new)
        l_ref[h] = alpha * l_ref[h] + jnp.sum(p, axis=0, keepdims=True)
        vt = vt_ref[kv * VT_ROWS:kv * VT_ROWS + HEAD_DIM, :]
        acc_ref[h] = alpha * acc_ref[h] + _dot(vt, p.astype(BF16))
        m_ref[h] = m_new

    @pl.when(ki == pl.num_programs(2) - 1)
    def _finish():
        o_ref[...] = _normalise_heads(lambda h: acc_ref[h], lambda h: l_ref[h], nw_ref)


def _attention(qt, k, vt, out_norm_w, k_norm_w, batch, seq):
    nw = out_norm_w[:, None]
    params = pltpu.CompilerParams(dimension_semantics=("parallel", "parallel", "arbitrary"),
                                  vmem_limit_bytes=VMEM_LIMIT_BYTES)
    out_shape = jax.ShapeDtypeStruct((batch * seq, ATT_WIDTH), F32)

    def specs(tq, tk):
        nq, nk = seq // tq, seq // tk
        return ([pl.BlockSpec((None, ATT_WIDTH, tq), lambda b, qi, ki: (b, 0, qi)),
                 pl.BlockSpec((tk, KV_WIDTH), lambda b, qi, ki: (b * nk + ki, 0)),
                 pl.BlockSpec((None, N_KV_HEADS * VT_ROWS, tk), lambda b, qi, ki: (b, 0, ki)),
                 pl.BlockSpec(nw.shape, lambda b, qi, ki: (0, 0))],
                pl.BlockSpec((tq, ATT_WIDTH), lambda b, qi, ki: (b * nq + qi, 0)), (batch, nq, nk))

    kmax = (HEAD_DIM ** 0.5 * 1.02) * jnp.max(jnp.abs(k_norm_w)).reshape(1, 1)
    tq, tk = ATT_Q_ROWS, ATT_K_ROWS
    nq = seq // tq
    in_specs, out_spec, grid = specs(tq, tk)
    att, l = pl.pallas_call(
        _attn_kernel,
        grid=grid,
        in_specs=[pl.BlockSpec((1, 1), lambda b, qi, ki: (0, 0))] + in_specs,
        out_specs=[out_spec, pl.BlockSpec((None, N_Q_HEADS, tq), lambda b, qi, ki: (b * nq + qi, 0, 0))],
        out_shape=[out_shape, jax.ShapeDtypeStruct((batch * nq, N_Q_HEADS, tq), F32)],
        scratch_shapes=[pltpu.VMEM((N_Q_HEADS, KV_WIDTH, tq), BF16),
                        pltpu.VMEM((N_Q_HEADS, 1, tq), F32),
                        pltpu.VMEM((N_Q_HEADS, VT_ROWS, tq), F32)],
        compiler_params=params,
        name="attention",
    )(kmax, qt, k, vt, nw)

    def safe():
        rows = ATT_SAFE_ROWS
        in_specs, out_spec, grid = specs(rows, rows)
        return pl.pallas_call(
            _attn_safe_kernel,
            grid=grid,
            in_specs=in_specs,
            out_specs=out_spec,
            out_shape=out_shape,
            scratch_shapes=[pltpu.VMEM((N_Q_HEADS, KV_WIDTH, rows), BF16),
                            pltpu.VMEM((N_Q_HEADS, 1, rows), F32),
                            pltpu.VMEM((N_Q_HEADS, 1, rows), F32),
                            pltpu.VMEM((N_Q_HEADS, HEAD_DIM, rows), F32)],
            compiler_params=params,
            name="attention_safe",
        )(qt, k, vt, nw)

    return lax.cond(jnp.min(l) >= ATT_MIN_SUM, lambda: att, safe)


def _hgrn_direct_scores(qf, kf, vf, b, reverse, ones_ref, rows_ref):
    n = HG_BLOCK
    b_ref, k_ref, v_ref = rows_ref
    b_ref[...] = b
    k_ref[...] = kf
    v_ref[...] = vf
    row = lax.broadcasted_iota(jnp.int32, (n, HG_WIDTH), 0)
    ones = ones_ref[...]

    def body(j, out):
        live = (row <= j) if reverse else (row >= j)
        decay = jnp.exp(jnp.where(live, b - b_ref[pl.ds(j, 1), :], -jnp.inf))
        hi, lo = _split_bf16(qf * k_ref[pl.ds(j, 1), :] * decay)
        return out + (_dot(hi, ones) + _dot(lo, ones)) * v_ref[pl.ds(j, 1), :]

    return lax.fori_loop(0, n, body, jnp.zeros((n, HG_WIDTH), F32))


def _hgrn_block(qh, lf, kk, v, st_ref, reverse, direct=None):
    n = HG_BLOCK
    row = lax.broadcasted_iota(jnp.int32, (n, n), 0)
    col = lax.broadcasted_iota(jnp.int32, (n, n), 1)
    tri = (col >= row) if reverse else (col <= row)
    tri_bf16 = jnp.where(tri, 1.0, 0.0).astype(BF16)
    first_head = col < HG_DIM
    same_head = (row < HG_DIM) == first_head

    lf_hi, lf_lo = _split_bf16(lf)
    b = _dot(tri_bf16, lf_hi) + _dot(tri_bf16, lf_lo)
    if reverse:
        b_end, b_mid = b[0:1, :], b[n // 2:n // 2 + 1, :]
    else:
        b_end, b_mid = b[n - 1:n, :], b[n // 2 - 1:n // 2, :]

    qf = qh.astype(F32)
    kf = kk.astype(F32)
    q_st = (qf * jnp.exp(b)).astype(BF16)
    k_st = (kf * jnp.exp(b_end - b)).astype(BF16)
    decay = jnp.exp(b_end)
    if direct is None:
        q_in = (qf * jnp.exp(b - b_mid)).astype(BF16)
        k_in = (kf * jnp.exp(b_mid - b)).astype(BF16)
        span = jnp.abs(b - b_mid)
        span = jnp.max(jnp.max(span, axis=0, keepdims=True), axis=1, keepdims=True)
    else:
        within = _hgrn_direct_scores(qf, kf, v.astype(F32), b, reverse, *direct)
        span = jnp.zeros((1, 1), F32)

    zero = jnp.zeros((), BF16)
    outs = []
    for p in range(HG_PAIRS):
        sl = slice(p * n, (p + 1) * n)
        v_p = v[:, sl]
        st = st_ref[p]
        o_p = _dot_nt(q_st[:, sl], st.astype(BF16))
        if direct is None:
            q_p, k_p = q_in[:, sl], k_in[:, sl]
            a0 = jnp.where(tri, _dot_nt(jnp.where(first_head, q_p, zero), k_p), 0.0).astype(BF16)
            a1 = jnp.where(tri, _dot_nt(jnp.where(first_head, zero, q_p), k_p), 0.0).astype(BF16)
            o_p = o_p + _dot(a0, jnp.where(first_head, v_p, zero)) + _dot(a1, jnp.where(first_head, zero, v_p))
        else:
            o_p = o_p + within[:, sl]
        outs.append(o_p)
        upd = _dot(v_p.astype(F32).T.astype(BF16), k_st[:, sl])
        st_ref[p] = st * decay[:, sl] + jnp.where(same_head, upd, 0.0)
    return jnp.concatenate(outs, axis=1), span


def _hgrn_scan(refs, of_ref, ob_ref, stf_ref, stb_ref, direct):
    qf_ref, lff_ref, kf_ref, vf_ref, qb_ref, lfb_ref, kb_ref, vb_ref = refs

    @pl.when(pl.program_id(1) == 0)
    def _init():
        stf_ref[...] = jnp.zeros(stf_ref.shape, F32)
        stb_ref[...] = jnp.zeros(stb_ref.shape, F32)

    nblk = qf_ref.shape[0] // HG_BLOCK
    spans = []
    for j in range(nblk):
        rows = slice(j * HG_BLOCK, (j + 1) * HG_BLOCK)
        of_ref[rows, :], span = _hgrn_block(qf_ref[rows, :], lff_ref[rows, :], kf_ref[rows, :],
                                            vf_ref[rows, :], stf_ref, False, direct)
        spans.append(span)
    for j in reversed(range(nblk)):
        rows = slice(j * HG_BLOCK, (j + 1) * HG_BLOCK)
        ob_ref[rows, :], span = _hgrn_block(qb_ref[rows, :], lfb_ref[rows, :], kb_ref[rows, :],
                                            vb_ref[rows, :], stb_ref, True, direct)
        spans.append(span)
    return functools.reduce(jnp.maximum, spans)


def _hgrn_kernel(*refs):
    *ins, of_ref, ob_ref, span_ref, stf_ref, stb_ref = refs
    span = _hgrn_scan(ins, of_ref, ob_ref, stf_ref, stb_ref, None)
    span_ref[...] = jnp.broadcast_to(span, span_ref.shape)


def _hgrn_safe_kernel(*refs):
    ones_ref, *ins, of_ref, ob_ref, stf_ref, stb_ref, b_ref, k_ref, v_ref = refs
    _hgrn_scan(ins, of_ref, ob_ref, stf_ref, stb_ref, (ones_ref, (b_ref, k_ref, v_ref)))


def _hgrn(hq, lff, kf, lfb, kb, hv, ones_blockdiag, batch, seq):
    tc = HG_ROWS
    nt = seq // tc
    fwd = pl.BlockSpec((tc, HG_WIDTH), lambda b, n: (b * nt + n, 0))
    bwd = pl.BlockSpec((tc, HG_WIDTH), lambda b, n: (b * nt + nt - 1 - n, 0))
    out = jax.ShapeDtypeStruct((batch * seq, HG_WIDTH), F32)
    state = pltpu.VMEM((HG_PAIRS, HG_BLOCK, HG_BLOCK), F32)
    params = pltpu.CompilerParams(dimension_semantics=("parallel", "arbitrary"),
                                  vmem_limit_bytes=VMEM_LIMIT_BYTES)
    operands = (hq, lff, kf, hv, hq, lfb, kb, hv)
    o_f, o_b, span = pl.pallas_call(
        _hgrn_kernel,
        grid=(batch, nt),
        in_specs=[fwd, fwd, fwd, fwd, bwd, bwd, bwd, bwd],
        out_specs=[fwd, bwd, pl.BlockSpec((None, 8, LANES), lambda b, n: (b * nt + n, 0, 0))],
        out_shape=[out, out, jax.ShapeDtypeStruct((batch * nt, 8, LANES), F32)],
        scratch_shapes=[state, state],
        compiler_params=params,
        name="hgrn",
    )(*operands)

    def safe():
        rows = pltpu.VMEM((HG_BLOCK, HG_WIDTH), F32)
        return tuple(pl.pallas_call(
            _hgrn_safe_kernel,
            grid=(batch, nt),
            in_specs=[pl.BlockSpec(ones_blockdiag.shape, lambda b, n: (0, 0)),
                      fwd, fwd, fwd, fwd, bwd, bwd, bwd, bwd],
            out_specs=[fwd, bwd],
            out_shape=[out, out],
            scratch_shapes=[state, state, rows, rows, rows],
            compiler_params=params,
            name="hgrn_safe",
        )(ones_blockdiag, *operands))

    return lax.cond(jnp.max(span) < HG_MAX_SPAN, lambda: (o_f, o_b), safe)


def _route(scores, biased):
    tm = scores.shape[1]
    neg_inf = -jnp.inf
    group_scores = []
    for g in range(N_GROUPS):
        blk = biased[g * EXPERTS_PER_GROUP:(g + 1) * EXPERTS_PER_GROUP, :]
        m1 = jnp.max(blk, axis=0, keepdims=True)
        n_max = jnp.sum(jnp.where(blk == m1, 1.0, 0.0), axis=0, keepdims=True)
        below = jnp.max(jnp.where(blk < m1, blk, neg_inf), axis=0, keepdims=True)
        group_scores.append(m1 + jnp.where(n_max >= 2.0, m1, below))
    gs = jnp.concatenate(group_scores, axis=0)

    gidx = lax.broadcasted_iota(jnp.int32, (N_GROUPS, tm), 0)
    rank = jnp.zeros((N_GROUPS, tm), F32)
    for g in range(N_GROUPS):
        other = gs[g:g + 1, :]
        ahead = jnp.where(other > gs, 1.0, jnp.where(other == gs, jnp.where(gidx > g, 1.0, 0.0), 0.0))
        rank = rank + ahead
    group_on = jnp.where(rank < float(TOPK_GROUPS), 1.0, 0.0)
    expert_on = jnp.concatenate(
        [jnp.broadcast_to(group_on[g:g + 1, :], (EXPERTS_PER_GROUP, tm)) for g in range(N_GROUPS)], axis=0)

    eidx = lax.broadcasted_iota(jnp.int32, (N_EXPERTS, tm), 0).astype(F32)
    work = jnp.where(expert_on > 0.0, biased, neg_inf)
    chosen = jnp.zeros((N_EXPERTS, tm), F32)
    for _ in range(TOP_K):
        best = jnp.max(work, axis=0, keepdims=True)
        first = jnp.min(jnp.where(work == best, eidx, float(N_EXPERTS)), axis=0, keepdims=True)
        pick = eidx == first
        chosen = jnp.where(pick, 1.0, chosen)
        work = jnp.where(pick, neg_inf, work)
    w = jnp.where(chosen > 0.0, scores, 0.0)
    return w / jnp.sum(w, axis=0, keepdims=True) * ROUTED_SCALE


def _mix_kernel(x_ref, att_ref, of_ref, ob_ref, g_ref, wout_ref, hnw_ref, lnw_ref, lnb_ref,
                wrhi_ref, wrlo_ref, rb_ref, ones_ref, x1_ref, gate_ref):
    o = of_ref[...] + ob_ref[...]
    o = o * lax.rsqrt(_group_mean_sq(o, ones_ref[...]) + RMS_EPS) * hnw_ref[...] * g_ref[...].astype(F32)
    mixed = jnp.concatenate([att_ref[...].astype(BF16), o.astype(BF16)], axis=1)
    h = DEEPNORM_ALPHA * x_ref[...] + _dot(mixed, wout_ref[...])
    x1 = _layer_norm(h, lnw_ref[...], lnb_ref[...])
    x1_ref[...] = x1

    x_hi, x_lo = _split_bf16(x1)
    w_hi, w_lo = wrhi_ref[...], wrlo_ref[...]
    logits = _dot_nt(w_hi, x_hi) + _dot_nt(w_hi, x_lo) + _dot_nt(w_lo, x_hi)
    scores = _sigmoid(logits)
    gate = _route(scores, scores + rb_ref[...])
    gate_ref[...] = jnp.concatenate([gate, jnp.zeros_like(gate)], axis=0).T


def _mix(x2, att, o_f, o_b, hg, w_out, hg_norm_w, ln_w, ln_b, w_router, router_bias, ones_blockdiag):
    t = x2.shape[0]
    tm = MIX_ROWS
    wr_hi, wr_lo = _split_bf16(w_router.T)
    args = (x2, att, o_f, o_b, hg, w_out, hg_norm_w[None, :], ln_w[None, :], ln_b[None, :],
            wr_hi, wr_lo, router_bias[:, None], ones_blockdiag)
    row_block = lambda w: pl.BlockSpec((tm, w), lambda i: (i, 0))
    full = lambda a: pl.BlockSpec(a.shape, lambda i: (0,) * a.ndim)
    return pl.pallas_call(
        _mix_kernel,
        grid=(t // tm,),
        in_specs=[row_block(D_MODEL), row_block(ATT_WIDTH), row_block(HG_WIDTH), row_block(HG_WIDTH),
                  row_block(HG_WIDTH)] + [full(a) for a in args[5:]],
        out_specs=[row_block(D_MODEL), row_block(LANES)],
        out_shape=[jax.ShapeDtypeStruct((t, D_MODEL), F32), jax.ShapeDtypeStruct((t, LANES), F32)],
        compiler_params=pltpu.CompilerParams(dimension_semantics=("parallel",),
                                             vmem_limit_bytes=VMEM_LIMIT_BYTES),
        name="mix",
    )(*args)


def _swiglu(xb, w_gate, w_up):
    hg = _dot(xb, w_gate)
    return hg * _sigmoid(hg) * _dot(xb, w_up)


def _moe_kernel(x1_ref, gate_ref, wg_ref, wu_ref, wd_ref, winv_ref, wsg_ref, wsu_ref, wsd_ref, lnw_ref, lnb_ref,
                o_ref, x8_ref, xinv_ref, acc_ref):
    step = pl.program_id(1)

    @pl.when(step == 0)
    def _init():
        x1 = x1_ref[...]
        peak = jnp.maximum(jnp.max(jnp.abs(x1), axis=1, keepdims=True), FP8_TINY)
        x8_ref[...] = (x1 * (FP8_PEAK / peak)).astype(FP8)
        xinv_ref[...] = peak * (1.0 / FP8_PEAK)
        xb = x1.astype(BF16)
        acc_ref[...] = _dot(_swiglu(xb, wsg_ref[...], wsu_ref[...]).astype(BF16), wsd_ref[...])

    x8 = x8_ref[...]
    xinv = xinv_ref[...]
    lane = lax.broadcasted_iota(jnp.int32, gate_ref.shape, 1)
    hidden = []
    for j in range(MOE_GROUP):
        hg = _dot(x8, wg_ref[j]) * (xinv * winv_ref[j, 0:1, 0:1])
        g = jnp.sum(jnp.where(lane == step * MOE_GROUP + j, gate_ref[...], 0.0), axis=1, keepdims=True)
        row_scale = g * xinv * (winv_ref[j, 0:1, 1:2] * winv_ref[j, 0:1, 2:3])
        hidden.append(hg * _sigmoid(hg) * _dot(x8, wu_ref[j]) * row_scale)
    hidden = jnp.concatenate(hidden, axis=1)
    peak = jnp.maximum(jnp.max(jnp.abs(hidden), axis=1, keepdims=True), FP8_TINY)
    h8 = (hidden * (FP8_PEAK / peak)).astype(FP8)
    w_down = wd_ref[...].reshape(MOE_GROUP * EXPERT_HIDDEN, D_MODEL)
    acc_ref[...] += _dot(h8, w_down) * (peak * (1.0 / FP8_PEAK))

    @pl.when(step == pl.num_programs(1) - 1)
    def _finish():
        h = DEEPNORM_ALPHA * x1_ref[...] + acc_ref[...]
        o_ref[...] = _layer_norm(h, lnw_ref[...], lnb_ref[...])


def _quantize_experts(w):
    peak = jnp.maximum(jnp.max(jnp.abs(w), axis=(1, 2), keepdims=True), FP8_TINY)
    return (w * (FP8_PEAK / peak)).astype(FP8), peak[:, 0, 0] * (1.0 / FP8_PEAK)


def _moe(x1, gate, w_eg8, w_eu8, w_ed, w_inv, w_sg, w_su, w_sd, ln_w, ln_b):
    t = x1.shape[0]
    tm = MOE_ROWS
    args = (x1, gate, w_eg8, w_eu8, w_ed, w_inv, w_sg, w_su, w_sd, ln_w[None, :], ln_b[None, :])
    row_block = lambda w: pl.BlockSpec((tm, w), lambda i, g: (i, 0))
    full = lambda a: pl.BlockSpec(a.shape, lambda i, g: (0,) * a.ndim)
    experts = lambda a: pl.BlockSpec((MOE_GROUP,) + a.shape[1:], lambda i, g: (g, 0, 0))
    return pl.pallas_call(
        _moe_kernel,
        grid=(t // tm, N_EXPERTS // MOE_GROUP),
        in_specs=[row_block(D_MODEL), row_block(LANES), experts(w_eg8), experts(w_eu8), experts(w_ed),
                  experts(w_inv)] + [full(a) for a in args[6:]],
        out_specs=row_block(D_MODEL),
        out_shape=jax.ShapeDtypeStruct((t, D_MODEL), F32),
        scratch_shapes=[pltpu.VMEM((tm, D_MODEL), FP8), pltpu.VMEM((tm, 1), F32),
                        pltpu.VMEM((tm, D_MODEL), F32)],
        compiler_params=pltpu.CompilerParams(dimension_semantics=("parallel", "arbitrary"),
                                             vmem_limit_bytes=VMEM_LIMIT_BYTES),
        name="moe",
    )(*args)


def _trunk(x, p):
    batch, seq, _ = x.shape
    x2 = x.reshape(batch * seq, D_MODEL)
    qt, k, vt, hq, lff, kf, lfb, kb, hv, hg = _project(
        x2, batch, seq, p["w_in"], p["q_norm_w"], p["k_norm_w"], p["lb"], p["ones"])
    att = _attention(qt, k, vt, p["attn_out_norm_w"], p["k_norm_w"], batch, seq)
    o_f, o_b = _hgrn(hq, lff, kf, lfb, kb, hv, p["ones"], batch, seq)
    x1, gate = _mix(x2, att, o_f, o_b, hg, p["w_out"], p["hgrn_out_norm_w"], p["ln1_w"], p["ln1_b"],
                    p["w_router"], p["router_bias"], p["ones"])
    y = _moe(x1, gate, p["w_exp_gate"], p["w_exp_up"], p["w_exp_down"], p["w_exp_inv"], p["w_sh_gate"],
             p["w_sh_up"], p["w_sh_down"], p["ln2_w"], p["ln2_b"])
    return y.reshape(batch, seq, D_MODEL)


def _prepare(w_in, q_norm_w, k_norm_w, attn_out_norm_w, hgrn_lb_logits, hgrn_out_norm_w, w_out, ln1_w, ln1_b, w_router, router_bias, w_exp_gate, w_exp_up, w_exp_down, w_sh_gate, w_sh_up, w_sh_down, ln2_w, ln2_b):
    layer = 0
    lb_all = jnp.cumsum(jax.nn.softmax(hgrn_lb_logits.astype(F32), axis=0), axis=0)
    head_of = jnp.arange(ATT_WIDTH) // HEAD_DIM
    w_eg8, eg_inv = _quantize_experts(w_exp_gate[layer])
    w_eu8, eu_inv = _quantize_experts(w_exp_up[layer])
    w_ed8, ed_inv = _quantize_experts(w_exp_down[layer])
    w_inv = jnp.zeros((N_EXPERTS, 8, LANES), F32)
    for lane, inv in enumerate((eg_inv, eu_inv, ed_inv)):
        w_inv = w_inv.at[:, :, lane].set(inv[:, None])
    return {
        "w_in": w_in[layer].astype(BF16),
        "q_norm_w": q_norm_w[layer], "k_norm_w": k_norm_w[layer],
        "attn_out_norm_w": attn_out_norm_w[layer],
        "lb": lb_all[layer],
        "hgrn_out_norm_w": hgrn_out_norm_w[layer],
        "w_out": w_out[layer].astype(BF16),
        "ln1_w": ln1_w[layer], "ln1_b": ln1_b[layer],
        "w_router": w_router[layer], "router_bias": router_bias[layer],
        "w_exp_gate": w_eg8, "w_exp_up": w_eu8, "w_exp_down": w_ed8, "w_exp_inv": w_inv,
        "w_sh_gate": w_sh_gate[layer].astype(BF16), "w_sh_up": w_sh_up[layer].astype(BF16),
        "w_sh_down": w_sh_down[layer].astype(BF16),
        "ln2_w": ln2_w[layer], "ln2_b": ln2_b[layer],
        "ones": (head_of[:, None] == head_of[None, :]).astype(BF16),
    }


def kernel(x_prompt, x_sample, w_in, q_norm_w, k_norm_w, attn_out_norm_w, hgrn_lb_logits, hgrn_out_norm_w, w_out, ln1_w, ln1_b, w_router, router_bias, w_exp_gate, w_exp_up, w_exp_down, w_sh_gate, w_sh_up, w_sh_down, ln2_w, ln2_b):
    p = _prepare(w_in, q_norm_w, k_norm_w, attn_out_norm_w, hgrn_lb_logits, hgrn_out_norm_w, w_out, ln1_w, ln1_b, w_router, router_bias, w_exp_gate, w_exp_up, w_exp_down, w_sh_gate, w_sh_up, w_sh_down, ln2_w, ln2_b)
    return _trunk(x_prompt, p), _trunk(x_sample, p)
```

```python
import functools

import jax
import jax.numpy as jnp
from jax import lax
from jax.experimental import pallas as pl
from jax.experimental.pallas import tpu as pltpu

F32 = jnp.float32
BF16 = jnp.bfloat16
FP8 = jnp.float8_e4m3fn
FP8_PEAK = 256.0
FP8_TINY = 1e-30

D_MODEL = 1024
N_Q_HEADS = 8
N_KV_HEADS = 2
HEAD_DIM = 64
Q_PER_KV = N_Q_HEADS // N_KV_HEADS
ATT_WIDTH = N_Q_HEADS * HEAD_DIM
KV_WIDTH = N_KV_HEADS * HEAD_DIM
ROPE_THETA = 10000.0
ROPE_AXIS_DIM = HEAD_DIM // 2
ROPE_HALF = ROPE_AXIS_DIM // 2
GRID_W = 64
HG_HEADS = 8
HG_DIM = 64
HG_WIDTH = HG_HEADS * HG_DIM
HG_BLOCK = 128
HG_PAIRS = HG_WIDTH // HG_BLOCK
N_EXPERTS = 64
N_GROUPS = 8
EXPERTS_PER_GROUP = N_EXPERTS // N_GROUPS
TOPK_GROUPS = 4
TOP_K = 8
EXPERT_HIDDEN = 256
ROUTED_SCALE = 2.5
DEPTH = 1
DEEPNORM_ALPHA = (2.0 * DEPTH) ** 0.25
RMS_EPS = 1e-6
LN_EPS = 1e-5

_OFF_Q = 0
_OFF_K = _OFF_Q + ATT_WIDTH
_OFF_V = _OFF_K + KV_WIDTH
_OFF_HQ = _OFF_V + KV_WIDTH
_OFF_HFF = _OFF_HQ + HG_WIDTH
_OFF_HFB = _OFF_HFF + HG_WIDTH
_OFF_HI = _OFF_HFB + HG_WIDTH
_OFF_HG = _OFF_HI + HG_WIDTH
IN_WIDTH = _OFF_HG + HG_WIDTH

VMEM_LIMIT_BYTES = 56 * 1024 * 1024
LANES = 128

LOG2_E = 1.4426950408889634
Q_SCALE = HEAD_DIM ** -0.5 * LOG2_E
VT_ROWS = HEAD_DIM + 16
ATT_MIN_SUM = 2.0 ** -80

PROJ_ROWS = 512
ATT_Q_ROWS = 1024
ATT_K_ROWS = 2048
ATT_SUB_K = 256
ATT_SUB_Q = 1024
ATT_SAFE_ROWS = 512
HG_ROWS = 512
HG_MAX_SPAN = 80.0
MIX_ROWS = 512
MOE_ROWS = 1024
MOE_GROUP = 4


def _dot(a, b):
    return jnp.dot(a, b, preferred_element_type=F32)


def _dot_nt(a, b):
    return lax.dot_general(a, b, (((1,), (1,)), ((), ())), preferred_element_type=F32)


def _split_bf16(x):
    hi = x.astype(BF16)
    lo = (x - hi.astype(F32)).astype(BF16)
    return hi, lo


def _sigmoid(x):
    return 1.0 / (1.0 + jnp.exp(-x))


def _group_mean_sq(x, ones_blockdiag):
    hi, lo = _split_bf16(x * x)
    return (_dot(hi, ones_blockdiag) + _dot(lo, ones_blockdiag)) * (1.0 / HEAD_DIM)


def _layer_norm(h, w, b):
    mu = jnp.mean(h, axis=-1, keepdims=True)
    hc = h - mu
    var = jnp.mean(hc * hc, axis=-1, keepdims=True)
    return hc * lax.rsqrt(var + LN_EPS) * w + b


def _rope(x, cos, sin_lo, sin_hi):
    n = x.shape[1]
    reps = n // LANES
    cos = jnp.concatenate([cos] * reps, axis=1)
    sin_lo = jnp.concatenate([sin_lo] * reps, axis=1)
    sin_hi = jnp.concatenate([sin_hi] * reps, axis=1)
    return (x * cos + pltpu.roll(x, n - ROPE_HALF, 1) * sin_lo
            + pltpu.roll(x, ROPE_HALF, 1) * sin_hi)


def _proj_kernel(x_ref, w_ref, cos_ref, sinlo_ref, sinhi_ref, qw_ref, kw_ref, lb_ref, ones_ref,
                 qt_ref, k_ref, vt_ref, hq_ref, lff_ref, kf_ref, lfb_ref, kb_ref, hv_ref, hg_ref):
    xb = x_ref[...].astype(BF16)

    def seg(off, width):
        return _dot(xb, w_ref[:, off:off + width])

    cos, sin_lo, sin_hi = cos_ref[...], sinlo_ref[...], sinhi_ref[...]

    q = seg(_OFF_Q, ATT_WIDTH)
    q = q * lax.rsqrt(_group_mean_sq(q, ones_ref[...]) + RMS_EPS) * qw_ref[...]
    q = _rope(q, cos, sin_lo, sin_hi) * Q_SCALE
    qt_ref[...] = q.T.astype(BF16)

    k = seg(_OFF_K, KV_WIDTH)
    k = k * lax.rsqrt(_group_mean_sq(k, ones_ref[0:KV_WIDTH, 0:KV_WIDTH]) + RMS_EPS) * kw_ref[...]
    k_ref[...] = _rope(k, cos, sin_lo, sin_hi).astype(BF16)

    vt = seg(_OFF_V, KV_WIDTH).T
    ones_rows = jnp.ones((VT_ROWS - HEAD_DIM, vt.shape[1]), F32)
    vt_ref[...] = jnp.concatenate(
        [piece for kv in range(N_KV_HEADS) for piece in (vt[kv * HEAD_DIM:(kv + 1) * HEAD_DIM, :], ones_rows)],
        axis=0).astype(BF16)

    hq = seg(_OFF_HQ, HG_WIDTH)
    hq_ref[...] = (hq * _sigmoid(hq)).astype(BF16)

    for off, row, lf_ref, kk_ref in ((_OFF_HFF, 0, lff_ref, kf_ref), (_OFF_HFB, 1, lfb_ref, kb_ref)):
        z = seg(off, HG_WIDTH)
        lb = lb_ref[row:row + 1, :]
        lf_ref[...] = jnp.log(lb + (1.0 - lb) * _sigmoid(z))
        kk_ref[...] = ((1.0 - lb) * _sigmoid(-z)).astype(BF16)

    hv_ref[...] = seg(_OFF_HI, HG_WIDTH).astype(BF16)
    hg = seg(_OFF_HG, HG_WIDTH)
    hg_ref[...] = (hg * _sigmoid(hg)).astype(BF16)


def _rope_tables(seq):
    rows = seq // GRID_W
    inv_freq = ROPE_THETA ** (-jnp.arange(0, ROPE_AXIS_DIM, 2, dtype=F32) / ROPE_AXIS_DIM)
    ang_r = jnp.arange(rows, dtype=F32)[:, None] * inv_freq
    ang_c = jnp.arange(GRID_W, dtype=F32)[:, None] * inv_freq
    lower = (jnp.arange(LANES) % ROPE_AXIS_DIM) < ROPE_HALF

    def table(fn):
        by_row = jnp.broadcast_to(fn(ang_r)[:, None, :], (rows, GRID_W, ROPE_HALF))
        by_col = jnp.broadcast_to(fn(ang_c)[None, :, :], (rows, GRID_W, ROPE_HALF))
        head = jnp.concatenate([by_row, by_row, by_col, by_col], axis=-1).reshape(seq, HEAD_DIM)
        return jnp.concatenate([head] * (LANES // HEAD_DIM), axis=-1)

    cos, sin = table(jnp.cos), table(jnp.sin)
    return cos, jnp.where(lower, -sin, 0.0), jnp.where(lower, 0.0, sin)


def _project(x2, batch, seq, w_in, q_norm_w, k_norm_w, lb, ones_blockdiag):
    t = batch * seq
    tm = PROJ_ROWS
    nt = seq // tm
    cos, sin_lo, sin_hi = _rope_tables(seq)
    qw = jnp.tile(q_norm_w, N_Q_HEADS)[None, :]
    kw = jnp.tile(k_norm_w, N_KV_HEADS)[None, :]

    row_block = lambda w: pl.BlockSpec((tm, w), lambda i: (i, 0))
    pos_block = pl.BlockSpec((tm, LANES), lambda i: (i % nt, 0))
    full = lambda a: pl.BlockSpec(a.shape, lambda i: (0,) * a.ndim)
    t_block = lambda w: pl.BlockSpec((None, w, tm), lambda i: (i // nt, 0, i % nt))

    hg_bf16 = jax.ShapeDtypeStruct((t, HG_WIDTH), BF16)
    hg_f32 = jax.ShapeDtypeStruct((t, HG_WIDTH), F32)
    return pl.pallas_call(
        _proj_kernel,
        grid=(t // tm,),
        in_specs=[row_block(D_MODEL), full(w_in), pos_block, pos_block, pos_block,
                  full(qw), full(kw), full(lb), full(ones_blockdiag)],
        out_specs=[t_block(ATT_WIDTH), row_block(KV_WIDTH), t_block(N_KV_HEADS * VT_ROWS),
                   row_block(HG_WIDTH), row_block(HG_WIDTH), row_block(HG_WIDTH),
                   row_block(HG_WIDTH), row_block(HG_WIDTH), row_block(HG_WIDTH), row_block(HG_WIDTH)],
        out_shape=[jax.ShapeDtypeStruct((batch, ATT_WIDTH, seq), BF16),
                   jax.ShapeDtypeStruct((t, KV_WIDTH), BF16),
                   jax.ShapeDtypeStruct((batch, N_KV_HEADS * VT_ROWS, seq), BF16),
                   hg_bf16, hg_f32, hg_bf16, hg_f32, hg_bf16, hg_bf16, hg_bf16],
        compiler_params=pltpu.CompilerParams(dimension_semantics=("parallel",),
                                             vmem_limit_bytes=VMEM_LIMIT_BYTES),
        name="proj",
    )(x2, w_in, cos, sin_lo, sin_hi, qw, kw, lb, ones_blockdiag)


def _load_queries(qt_ref, qz_ref):
    tq = qt_ref.shape[1]
    zeros = jnp.zeros((HEAD_DIM, tq), BF16)
    for h in range(N_Q_HEADS):
        qh = qt_ref[h * HEAD_DIM:(h + 1) * HEAD_DIM, :]
        qz_ref[h] = jnp.concatenate([qh, zeros] if h < Q_PER_KV else [zeros, qh], axis=0)


def _normalise_heads(numer, denom, nw_ref):
    outs = []
    for h in range(N_Q_HEADS):
        o = numer(h) / denom(h)
        ms = jnp.mean(o * o, axis=0, keepdims=True)
        outs.append(o * lax.rsqrt(ms + RMS_EPS) * nw_ref[h * HEAD_DIM:(h + 1) * HEAD_DIM, :])
    return jnp.concatenate(outs, axis=0).T


def _attn_kernel(kmax_ref, qt_ref, k_ref, vt_ref, nw_ref, o_ref, l_ref, qz_ref, c_ref, acc_ref):
    ki = pl.program_id(2)
    tq = qt_ref.shape[1]
    tk = k_ref.shape[0]

    @pl.when(ki == 0)
    def _init():
        _load_queries(qt_ref, qz_ref)
        for h in range(N_Q_HEADS):
            qf = qt_ref[h * HEAD_DIM:(h + 1) * HEAD_DIM, :].astype(F32)
            c_ref[h] = jnp.sqrt(jnp.sum(qf * qf, axis=0, keepdims=True)) * kmax_ref[...]
        acc_ref[...] = jnp.zeros(acc_ref.shape, F32)

    def finish_tile(tile):
        s, h, keys, qsl = tile
        kv = h // Q_PER_KV
        p = jnp.exp2(s - c_ref[h, :, qsl]).astype(BF16)
        acc_ref[h, :, qsl] += _dot(vt_ref[kv * VT_ROWS:(kv + 1) * VT_ROWS, keys], p)

    pending = None
    for h in range(N_Q_HEADS):
        for js in range(tk // ATT_SUB_K):
            keys = slice(js * ATT_SUB_K, (js + 1) * ATT_SUB_K)
            for qs in range(tq // ATT_SUB_Q):
                qsl = slice(qs * ATT_SUB_Q, (qs + 1) * ATT_SUB_Q)
                s = _dot(k_ref[keys, :], qz_ref[h, :, qsl])
                if pending is not None:
                    finish_tile(pending)
                pending = (s, h, keys, qsl)
    finish_tile(pending)

    @pl.when(ki == pl.num_programs(2) - 1)
    def _finish():
        o_ref[...] = _normalise_heads(lambda h: acc_ref[h, 0:HEAD_DIM, :],
                                      lambda h: acc_ref[h, HEAD_DIM:HEAD_DIM + 1, :], nw_ref)
        for h in range(N_Q_HEADS):
            l_ref[h:h + 1, :] = acc_ref[h, HEAD_DIM:HEAD_DIM + 1, :]


def _attn_safe_kernel(qt_ref, k_ref, vt_ref, nw_ref, o_ref, qz_ref, m_ref, l_ref, acc_ref):
    ki = pl.program_id(2)

    @pl.when(ki == 0)
    def _init():
        _load_queries(qt_ref, qz_ref)
        m_ref[...] = jnp.full(m_ref.shape, -jnp.inf, F32)
        l_ref[...] = jnp.zeros(l_ref.shape, F32)
        acc_ref[...] = jnp.zeros(acc_ref.shape, F32)

    k = k_ref[...]
    for h in range(N_Q_HEADS):
        kv = h // Q_PER_KV
        s = _dot(k, qz_ref[h])
        m_old = m_ref[h]
        m_new = jnp.maximum(m_old, jnp.max(s, axis=0, keepdims=True))
        alpha = jnp.exp2(m_old - m_new)
        p = jnp.exp2(s - m_new)
        l_ref[h] = alpha * l_ref[h] + jnp.sum(p, axis=0, keepdims=True)
        vt = vt_ref[kv * VT_ROWS:kv * VT_ROWS + HEAD_DIM, :]
        acc_ref[h] = alpha * acc_ref[h] + _dot(vt, p.astype(BF16))
        m_ref[h] = m_new

    @pl.when(ki == pl.num_programs(2) - 1)
    def _finish():
        o_ref[...] = _normalise_heads(lambda h: acc_ref[h], lambda h: l_ref[h], nw_ref)


def _attention(qt, k, vt, out_norm_w, k_norm_w, batch, seq):
    nw = out_norm_w[:, None]
    params = pltpu.CompilerParams(dimension_semantics=("parallel", "parallel", "arbitrary"),
                                  vmem_limit_bytes=VMEM_LIMIT_BYTES)
    out_shape = jax.ShapeDtypeStruct((batch * seq, ATT_WIDTH), F32)

    def specs(tq, tk):
        nq, nk = seq // tq, seq // tk
        return ([pl.BlockSpec((None, ATT_WIDTH, tq), lambda b, qi, ki: (b, 0, qi)),
                 pl.BlockSpec((tk, KV_WIDTH), lambda b, qi, ki: (b * nk + ki, 0)),
                 pl.BlockSpec((None, N_KV_HEADS * VT_ROWS, tk), lambda b, qi, ki: (b, 0, ki)),
                 pl.BlockSpec(nw.shape, lambda b, qi, ki: (0, 0))],
                pl.BlockSpec((tq, ATT_WIDTH), lambda b, qi, ki: (b * nq + qi, 0)), (batch, nq, nk))

    kmax = (HEAD_DIM ** 0.5 * 1.02) * jnp.max(jnp.abs(k_norm_w)).reshape(1, 1)
    tq, tk = ATT_Q_ROWS, ATT_K_ROWS
    nq = seq // tq
    in_specs, out_spec, grid = specs(tq, tk)
    att, l = pl.pallas_call(
        _attn_kernel,
        grid=grid,
        in_specs=[pl.BlockSpec((1, 1), lambda b, qi, ki: (0, 0))] + in_specs,
        out_specs=[out_spec, pl.BlockSpec((None, N_Q_HEADS, tq), lambda b, qi, ki: (b * nq + qi, 0, 0))],
        out_shape=[out_shape, jax.ShapeDtypeStruct((batch * nq, N_Q_HEADS, tq), F32)],
        scratch_shapes=[pltpu.VMEM((N_Q_HEADS, KV_WIDTH, tq), BF16),
                        pltpu.VMEM((N_Q_HEADS, 1, tq), F32),
                        pltpu.VMEM((N_Q_HEADS, VT_ROWS, tq), F32)],
        compiler_params=params,
        name="attention",
    )(kmax, qt, k, vt, nw)

    def safe():
        rows = ATT_SAFE_ROWS
        in_specs, out_spec, grid = specs(rows, rows)
        return pl.pallas_call(
            _attn_safe_kernel,
            grid=grid,
            in_specs=in_specs,
            out_specs=out_spec,
            out_shape=out_shape,
            scratch_shapes=[pltpu.VMEM((N_Q_HEADS, KV_WIDTH, rows), BF16),
                            pltpu.VMEM((N_Q_HEADS, 1, rows), F32),
                            pltpu.VMEM((N_Q_HEADS, 1, rows), F32),
                            pltpu.VMEM((N_Q_HEADS, HEAD_DIM, rows), F32)],
            compiler_params=params,
            name="attention_safe",
        )(qt, k, vt, nw)

    return lax.cond(jnp.min(l) >= ATT_MIN_SUM, lambda: att, safe)


def _hgrn_direct_scores(qf, kf, vf, b, reverse, ones_ref, rows_ref):
    n = HG_BLOCK
    b_ref, k_ref, v_ref = rows_ref
    b_ref[...] = b
    k_ref[...] = kf
    v_ref[...] = vf
    row = lax.broadcasted_iota(jnp.int32, (n, HG_WIDTH), 0)
    ones = ones_ref[...]

    def body(j, out):
        live = (row <= j) if reverse else (row >= j)
        decay = jnp.exp(jnp.where(live, b - b_ref[pl.ds(j, 1), :], -jnp.inf))
        hi, lo = _split_bf16(qf * k_ref[pl.ds(j, 1), :] * decay)
        return out + (_dot(hi, ones) + _dot(lo, ones)) * v_ref[pl.ds(j, 1), :]

    return lax.fori_loop(0, n, body, jnp.zeros((n, HG_WIDTH), F32))


def _hgrn_block(qh, lf, kk, v, st_ref, reverse, direct=None):
    n = HG_BLOCK
    row = lax.broadcasted_iota(jnp.int32, (n, n), 0)
    col = lax.broadcasted_iota(jnp.int32, (n, n), 1)
    tri = (col >= row) if reverse else (col <= row)
    tri_bf16 = jnp.where(tri, 1.0, 0.0).astype(BF16)
    first_head = col < HG_DIM
    same_head = (row < HG_DIM) == first_head

    lf_hi, lf_lo = _split_bf16(lf)
    b = _dot(tri_bf16, lf_hi) + _dot(tri_bf16, lf_lo)
    if reverse:
        b_end, b_mid = b[0:1, :], b[n // 2:n // 2 + 1, :]
    else:
        b_end, b_mid = b[n - 1:n, :], b[n // 2 - 1:n // 2, :]

    qf = qh.astype(F32)
    kf = kk.astype(F32)
    q_st = (qf * jnp.exp(b)).astype(BF16)
    k_st = (kf * jnp.exp(b_end - b)).astype(BF16)
    decay = jnp.exp(b_end)
    if direct is None:
        q_in = (qf * jnp.exp(b - b_mid)).astype(BF16)
        k_in = (kf * jnp.exp(b_mid - b)).astype(BF16)
        span = jnp.abs(b - b_mid)
        span = jnp.max(jnp.max(span, axis=0, keepdims=True), axis=1, keepdims=True)
    else:
        within = _hgrn_direct_scores(qf, kf, v.astype(F32), b, reverse, *direct)
        span = jnp.zeros((1, 1), F32)

    zero = jnp.zeros((), BF16)
    outs = []
    for p in range(HG_PAIRS):
        sl = slice(p * n, (p + 1) * n)
        v_p = v[:, sl]
        st = st_ref[p]
        o_p = _dot_nt(q_st[:, sl], st.astype(BF16))
        if direct is None:
            q_p, k_p = q_in[:, sl], k_in[:, sl]
            a0 = jnp.where(tri, _dot_nt(jnp.where(first_head, q_p, zero), k_p), 0.0).astype(BF16)
            a1 = jnp.where(tri, _dot_nt(jnp.where(first_head, zero, q_p), k_p), 0.0).astype(BF16)
            o_p = o_p + _dot(a0, jnp.where(first_head, v_p, zero)) + _dot(a1, jnp.where(first_head, zero, v_p))
        else:
            o_p = o_p + within[:, sl]
        outs.append(o_p)
        upd = _dot(v_p.astype(F32).T.astype(BF16), k_st[:, sl])
        st_ref[p] = st * decay[:, sl] + jnp.where(same_head, upd, 0.0)
    return jnp.concatenate(outs, axis=1), span


def _hgrn_scan(refs, of_ref, ob_ref, stf_ref, stb_ref, direct):
    qf_ref, lff_ref, kf_ref, vf_ref, qb_ref, lfb_ref, kb_ref, vb_ref = refs

    @pl.when(pl.program_id(1) == 0)
    def _init():
        stf_ref[...] = jnp.zeros(stf_ref.shape, F32)
        stb_ref[...] = jnp.zeros(stb_ref.shape, F32)

    nblk = qf_ref.shape[0] // HG_BLOCK
    spans = []
    for j in range(nblk):
        rows = slice(j * HG_BLOCK, (j + 1) * HG_BLOCK)
        of_ref[rows, :], span = _hgrn_block(qf_ref[rows, :], lff_ref[rows, :], kf_ref[rows, :],
                                            vf_ref[rows, :], stf_ref, False, direct)
        spans.append(span)
    for j in reversed(range(nblk)):
        rows = slice(j * HG_BLOCK, (j + 1) * HG_BLOCK)
        ob_ref[rows, :], span = _hgrn_block(qb_ref[rows, :], lfb_ref[rows, :], kb_ref[rows, :],
                                            vb_ref[rows, :], stb_ref, True, direct)
        spans.append(span)
    return functools.reduce(jnp.maximum, spans)


def _hgrn_kernel(*refs):
    *ins, of_ref, ob_ref, span_ref, stf_ref, stb_ref = refs
    span = _hgrn_scan(ins, of_ref, ob_ref, stf_ref, stb_ref, None)
    span_ref[...] = jnp.broadcast_to(span, span_ref.shape)


def _hgrn_safe_kernel(*refs):
    ones_ref, *ins, of_ref, ob_ref, stf_ref, stb_ref, b_ref, k_ref, v_ref = refs
    _hgrn_scan(ins, of_ref, ob_ref, stf_ref, stb_ref, (ones_ref, (b_ref, k_ref, v_ref)))


def _hgrn(hq, lff, kf, lfb, kb, hv, ones_blockdiag, batch, seq):
    tc = HG_ROWS
    nt = seq // tc
    fwd = pl.BlockSpec((tc, HG_WIDTH), lambda b, n: (b * nt + n, 0))
    bwd = pl.BlockSpec((tc, HG_WIDTH), lambda b, n: (b * nt + nt - 1 - n, 0))
    out = jax.ShapeDtypeStruct((batch * seq, HG_WIDTH), F32)
    state = pltpu.VMEM((HG_PAIRS, HG_BLOCK, HG_BLOCK), F32)
    params = pltpu.CompilerParams(dimension_semantics=("parallel", "arbitrary"),
                                  vmem_limit_bytes=VMEM_LIMIT_BYTES)
    operands = (hq, lff, kf, hv, hq, lfb, kb, hv)
    o_f, o_b, span = pl.pallas_call(
        _hgrn_kernel,
        grid=(batch, nt),
        in_specs=[fwd, fwd, fwd, fwd, bwd, bwd, bwd, bwd],
        out_specs=[fwd, bwd, pl.BlockSpec((None, 8, LANES), lambda b, n: (b * nt + n, 0, 0))],
        out_shape=[out, out, jax.ShapeDtypeStruct((batch * nt, 8, LANES), F32)],
        scratch_shapes=[state, state],
        compiler_params=params,
        name="hgrn",
    )(*operands)

    def safe():
        rows = pltpu.VMEM((HG_BLOCK, HG_WIDTH), F32)
        return tuple(pl.pallas_call(
            _hgrn_safe_kernel,
            grid=(batch, nt),
            in_specs=[pl.BlockSpec(ones_blockdiag.shape, lambda b, n: (0, 0)),
                      fwd, fwd, fwd, fwd, bwd, bwd, bwd, bwd],
            out_specs=[fwd, bwd],
            out_shape=[out, out],
            scratch_shapes=[state, state, rows, rows, rows],
            compiler_params=params,
            name="hgrn_safe",
        )(ones_blockdiag, *operands))

    return lax.cond(jnp.max(span) < HG_MAX_SPAN, lambda: (o_f, o_b), safe)


def _route(scores, biased):
    tm = scores.shape[1]
    neg_inf = -jnp.inf
    group_scores = []
    for g in range(N_GROUPS):
        blk = biased[g * EXPERTS_PER_GROUP:(g + 1) * EXPERTS_PER_GROUP, :]
        m1 = jnp.max(blk, axis=0, keepdims=True)
        n_max = jnp.sum(jnp.where(blk == m1, 1.0, 0.0), axis=0, keepdims=True)
        below = jnp.max(jnp.where(blk < m1, blk, neg_inf), axis=0, keepdims=True)
        group_scores.append(m1 + jnp.where(n_max >= 2.0, m1, below))
    gs = jnp.concatenate(group_scores, axis=0)

    gidx = lax.broadcasted_iota(jnp.int32, (N_GROUPS, tm), 0)
    rank = jnp.zeros((N_GROUPS, tm), F32)
    for g in range(N_GROUPS):
        other = gs[g:g + 1, :]
        ahead = jnp.where(other > gs, 1.0, jnp.where(other == gs, jnp.where(gidx > g, 1.0, 0.0), 0.0))
        rank = rank + ahead
    group_on = jnp.where(rank < float(TOPK_GROUPS), 1.0, 0.0)
    expert_on = jnp.concatenate(
        [jnp.broadcast_to(group_on[g:g + 1, :], (EXPERTS_PER_GROUP, tm)) for g in range(N_GROUPS)], axis=0)

    eidx = lax.broadcasted_iota(jnp.int32, (N_EXPERTS, tm), 0).astype(F32)
    work = jnp.where(expert_on > 0.0, biased, neg_inf)
    chosen = jnp.zeros((N_EXPERTS, tm), F32)
    for _ in range(TOP_K):
        best = jnp.max(work, axis=0, keepdims=True)
        first = jnp.min(jnp.where(work == best, eidx, float(N_EXPERTS)), axis=0, keepdims=True)
        pick = eidx == first
        chosen = jnp.where(pick, 1.0, chosen)
        work = jnp.where(pick, neg_inf, work)
    w = jnp.where(chosen > 0.0, scores, 0.0)
    return w / jnp.sum(w, axis=0, keepdims=True) * ROUTED_SCALE


def _mix_kernel(x_ref, att_ref, of_ref, ob_ref, g_ref, wout_ref, hnw_ref, lnw_ref, lnb_ref,
                wrhi_ref, wrlo_ref, rb_ref, ones_ref, x1_ref, gate_ref):
    o = of_ref[...] + ob_ref[...]
    o = o * lax.rsqrt(_group_mean_sq(o, ones_ref[...]) + RMS_EPS) * hnw_ref[...] * g_ref[...].astype(F32)
    mixed = jnp.concatenate([att_ref[...].astype(BF16), o.astype(BF16)], axis=1)
    h = DEEPNORM_ALPHA * x_ref[...] + _dot(mixed, wout_ref[...])
    x1 = _layer_norm(h, lnw_ref[...], lnb_ref[...])
    x1_ref[...] = x1

    x_hi, x_lo = _split_bf16(x1)
    w_hi, w_lo = wrhi_ref[...], wrlo_ref[...]
    logits = _dot_nt(w_hi, x_hi) + _dot_nt(w_hi, x_lo) + _dot_nt(w_lo, x_hi)
    scores = _sigmoid(logits)
    gate = _route(scores, scores + rb_ref[...])
    gate_ref[...] = jnp.concatenate([gate, jnp.zeros_like(gate)], axis=0).T


def _mix(x2, att, o_f, o_b, hg, w_out, hg_norm_w, ln_w, ln_b, w_router, router_bias, ones_blockdiag):
    t = x2.shape[0]
    tm = MIX_ROWS
    wr_hi, wr_lo = _split_bf16(w_router.T)
    args = (x2, att, o_f, o_b, hg, w_out, hg_norm_w[None, :], ln_w[None, :], ln_b[None, :],
            wr_hi, wr_lo, router_bias[:, None], ones_blockdiag)
    row_block = lambda w: pl.BlockSpec((tm, w), lambda i: (i, 0))
    full = lambda a: pl.BlockSpec(a.shape, lambda i: (0,) * a.ndim)
    return pl.pallas_call(
        _mix_kernel,
        grid=(t // tm,),
        in_specs=[row_block(D_MODEL), row_block(ATT_WIDTH), row_block(HG_WIDTH), row_block(HG_WIDTH),
                  row_block(HG_WIDTH)] + [full(a) for a in args[5:]],
        out_specs=[row_block(D_MODEL), row_block(LANES)],
        out_shape=[jax.ShapeDtypeStruct((t, D_MODEL), F32), jax.ShapeDtypeStruct((t, LANES), F32)],
        compiler_params=pltpu.CompilerParams(dimension_semantics=("parallel",),
                                             vmem_limit_bytes=VMEM_LIMIT_BYTES),
        name="mix",
    )(*args)


def _swiglu(xb, w_gate, w_up):
    hg = _dot(xb, w_gate)
    return hg * _sigmoid(hg) * _dot(xb, w_up)


def _moe_kernel(x1_ref, gate_ref, wg_ref, wu_ref, wd_ref, winv_ref, wsg_ref, wsu_ref, wsd_ref, lnw_ref, lnb_ref,
                o_ref, x8_ref, xinv_ref, acc_ref):
    step = pl.program_id(1)

    @pl.when(step == 0)
    def _init():
        x1 = x1_ref[...]
        peak = jnp.maximum(jnp.max(jnp.abs(x1), axis=1, keepdims=True), FP8_TINY)
        x8_ref[...] = (x1 * (FP8_PEAK / peak)).astype(FP8)
        xinv_ref[...] = peak * (1.0 / FP8_PEAK)
        xb = x1.astype(BF16)
        acc_ref[...] = _dot(_swiglu(xb, wsg_ref[...], wsu_ref[...]).astype(BF16), wsd_ref[...])

    x8 = x8_ref[...]
    xinv = xinv_ref[...]
    lane = lax.broadcasted_iota(jnp.int32, gate_ref.shape, 1)
    hidden = []
    for j in range(MOE_GROUP):
        raw_gate = _dot(x8, wg_ref[j])
        g = jnp.sum(jnp.where(lane == step * MOE_GROUP + j, gate_ref[...], 0.0), axis=1, keepdims=True)
        gate_scale = xinv * winv_ref[j, 0:1, 0:1]
        sig = 1.0 / (1.0 + jnp.exp2(raw_gate * (gate_scale * -LOG2_E)))
        row_scale = gate_scale * g * xinv * (winv_ref[j, 0:1, 1:2] * winv_ref[j, 0:1, 2:3])
        hidden.append(raw_gate * sig * _dot(x8, wu_ref[j]) * row_scale)
    hidden = jnp.concatenate(hidden, axis=1)
    peak = jnp.maximum(jnp.max(jnp.abs(hidden), axis=1, keepdims=True), FP8_TINY)
    h8 = (hidden * (FP8_PEAK / peak)).astype(FP8)
    w_down = wd_ref[...].reshape(MOE_GROUP * EXPERT_HIDDEN, D_MODEL)
    acc_ref[...] += _dot(h8, w_down) * (peak * (1.0 / FP8_PEAK))

    @pl.when(step == pl.num_programs(1) - 1)
    def _finish():
        h = DEEPNORM_ALPHA * x1_ref[...] + acc_ref[...]
        o_ref[...] = _layer_norm(h, lnw_ref[...], lnb_ref[...])


def _quantize_kernel(wg_ref, wu_ref, wd_ref, wg8_ref, wu8_ref, wd8_ref, inv_ref):
    lane = lax.broadcasted_iota(jnp.int32, inv_ref.shape, 1)
    inv = jnp.zeros(inv_ref.shape, F32)
    for i, (w_ref, w8_ref) in enumerate(((wg_ref, wg8_ref), (wu_ref, wu8_ref), (wd_ref, wd8_ref))):
        w = w_ref[...]
        peak = jnp.max(jnp.max(jnp.abs(w), axis=0, keepdims=True), axis=1, keepdims=True)
        peak = jnp.maximum(peak, FP8_TINY)
        w8_ref[...] = (w * (FP8_PEAK / peak)).astype(FP8)
        inv = jnp.where(lane == i, peak * (1.0 / FP8_PEAK), inv)
    inv_ref[...] = inv


def _quantize_experts(w_gate, w_up, w_down):
    spec = lambda a: pl.BlockSpec((None,) + a.shape[1:], lambda e: (e, 0, 0))
    fp8 = lambda a: jax.ShapeDtypeStruct(a.shape, FP8)
    inv = jax.ShapeDtypeStruct((N_EXPERTS, 8, LANES), F32)
    return pl.pallas_call(
        _quantize_kernel,
        grid=(N_EXPERTS,),
        in_specs=[spec(w_gate), spec(w_up), spec(w_down)],
        out_specs=[spec(w_gate), spec(w_up), spec(w_down), spec(inv)],
        out_shape=[fp8(w_gate), fp8(w_up), fp8(w_down), inv],
        compiler_params=pltpu.CompilerParams(dimension_semantics=("parallel",),
                                             vmem_limit_bytes=VMEM_LIMIT_BYTES),
        name="quantize",
    )(w_gate, w_up, w_down)


def _moe(x1, gate, w_eg8, w_eu8, w_ed, w_inv, w_sg, w_su, w_sd, ln_w, ln_b):
    t = x1.shape[0]
    tm = MOE_ROWS
    args = (x1, gate, w_eg8, w_eu8, w_ed, w_inv, w_sg, w_su, w_sd, ln_w[None, :], ln_b[None, :])
    row_block = lambda w: pl.BlockSpec((tm, w), lambda i, g: (i, 0))
    full = lambda a: pl.BlockSpec(a.shape, lambda i, g: (0,) * a.ndim)
    experts = lambda a: pl.BlockSpec((MOE_GROUP,) + a.shape[1:], lambda i, g: (g, 0, 0))
    return pl.pallas_call(
        _moe_kernel,
        grid=(t // tm, N_EXPERTS // MOE_GROUP),
        in_specs=[row_block(D_MODEL), row_block(LANES), experts(w_eg8), experts(w_eu8), experts(w_ed),
                  experts(w_inv)] + [full(a) for a in args[6:]],
        out_specs=row_block(D_MODEL),
        out_shape=jax.ShapeDtypeStruct((t, D_MODEL), F32),
        scratch_shapes=[pltpu.VMEM((tm, D_MODEL), FP8), pltpu.VMEM((tm, 1), F32),
                        pltpu.VMEM((tm, D_MODEL), F32)],
        compiler_params=pltpu.CompilerParams(dimension_semantics=("parallel", "arbitrary"),
                                             vmem_limit_bytes=VMEM_LIMIT_BYTES),
        name="moe",
    )(*args)


def _trunk(x, p):
    batch, seq, _ = x.shape
    x2 = x.reshape(batch * seq, D_MODEL)
    qt, k, vt, hq, lff, kf, lfb, kb, hv, hg = _project(
        x2, batch, seq, p["w_in"], p["q_norm_w"], p["k_norm_w"], p["lb"], p["ones"])
    att = _attention(qt, k, vt, p["attn_out_norm_w"], p["k_norm_w"], batch, seq)
    o_f, o_b = _hgrn(hq, lff, kf, lfb, kb, hv, p["ones"], batch, seq)
    x1, gate = _mix(x2, att, o_f, o_b, hg, p["w_out"], p["hgrn_out_norm_w"], p["ln1_w"], p["ln1_b"],
                    p["w_router"], p["router_bias"], p["ones"])
    y = _moe(x1, gate, p["w_exp_gate"], p["w_exp_up"], p["w_exp_down"], p["w_exp_inv"], p["w_sh_gate"],
             p["w_sh_up"], p["w_sh_down"], p["ln2_w"], p["ln2_b"])
    return y.reshape(batch, seq, D_MODEL)


def _prepare(w_in, q_norm_w, k_norm_w, attn_out_norm_w, hgrn_lb_logits, hgrn_out_norm_w, w_out, ln1_w, ln1_b, w_router, router_bias, w_exp_gate, w_exp_up, w_exp_down, w_sh_gate, w_sh_up, w_sh_down, ln2_w, ln2_b):
    layer = 0
    lb_all = jnp.cumsum(jax.nn.softmax(hgrn_lb_logits.astype(F32), axis=0), axis=0)
    head_of = jnp.arange(ATT_WIDTH) // HEAD_DIM
    w_eg8, w_eu8, w_ed8, w_inv = _quantize_experts(w_exp_gate[layer], w_exp_up[layer], w_exp_down[layer])
    return {
        "w_in": w_in[layer].astype(BF16),
        "q_norm_w": q_norm_w[layer], "k_norm_w": k_norm_w[layer],
        "attn_out_norm_w": attn_out_norm_w[layer],
        "lb": lb_all[layer],
        "hgrn_out_norm_w": hgrn_out_norm_w[layer],
        "w_out": w_out[layer].astype(BF16),
        "ln1_w": ln1_w[layer], "ln1_b": ln1_b[layer],
        "w_router": w_router[layer], "router_bias": router_bias[layer],
        "w_exp_gate": w_eg8, "w_exp_up": w_eu8, "w_exp_down": w_ed8, "w_exp_inv": w_inv,
        "w_sh_gate": w_sh_gate[layer].astype(BF16), "w_sh_up": w_sh_up[layer].astype(BF16),
        "w_sh_down": w_sh_down[layer].astype(BF16),
        "ln2_w": ln2_w[layer], "ln2_b": ln2_b[layer],
        "ones": (head_of[:, None] == head_of[None, :]).astype(BF16),
    }


def kernel(x_prompt, x_sample, w_in, q_norm_w, k_norm_w, attn_out_norm_w, hgrn_lb_logits, hgrn_out_norm_w, w_out, ln1_w, ln1_b, w_router, router_bias, w_exp_gate, w_exp_up, w_exp_down, w_sh_gate, w_sh_up, w_sh_down, ln2_w, ln2_b):
    p = _prepare(w_in, q_norm_w, k_norm_w, attn_out_norm_w, hgrn_lb_logits, hgrn_out_norm_w, w_out, ln1_w, ln1_b, w_router, router_bias, w_exp_gate, w_exp_up, w_exp_down, w_sh_gate, w_sh_up, w_sh_down, ln2_w, ln2_b)
    return _trunk(x_prompt, p), _trunk(x_sample, p)
```

```python
import functools

import jax
import jax.numpy as jnp
from jax import lax
from jax.experimental import pallas as pl
from jax.experimental.pallas import tpu as pltpu

F32 = jnp.float32
BF16 = jnp.bfloat16
FP8 = jnp.float8_e4m3fn
FP8_PEAK = 256.0
FP8_TINY = 1e-30

D_MODEL = 1024
N_Q_HEADS = 8
N_KV_HEADS = 2
HEAD_DIM = 64
Q_PER_KV = N_Q_HEADS // N_KV_HEADS
ATT_WIDTH = N_Q_HEADS * HEAD_DIM
KV_WIDTH = N_KV_HEADS * HEAD_DIM
ROPE_THETA = 10000.0
ROPE_AXIS_DIM = HEAD_DIM // 2
ROPE_HALF = ROPE_AXIS_DIM // 2
GRID_W = 64
HG_HEADS = 8
HG_DIM = 64
HG_WIDTH = HG_HEADS * HG_DIM
HG_BLOCK = 128
HG_PAIRS = HG_WIDTH // HG_BLOCK
N_EXPERTS = 64
N_GROUPS = 8
EXPERTS_PER_GROUP = N_EXPERTS // N_GROUPS
TOPK_GROUPS = 4
TOP_K = 8
EXPERT_HIDDEN = 256
ROUTED_SCALE = 2.5
DEPTH = 1
DEEPNORM_ALPHA = (2.0 * DEPTH) ** 0.25
RMS_EPS = 1e-6
LN_EPS = 1e-5

_OFF_Q = 0
_OFF_K = _OFF_Q + ATT_WIDTH
_OFF_V = _OFF_K + KV_WIDTH
_OFF_HQ = _OFF_V + KV_WIDTH
_OFF_HFF = _OFF_HQ + HG_WIDTH
_OFF_HFB = _OFF_HFF + HG_WIDTH
_OFF_HI = _OFF_HFB + HG_WIDTH
_OFF_HG = _OFF_HI + HG_WIDTH
IN_WIDTH = _OFF_HG + HG_WIDTH

VMEM_LIMIT_BYTES = 56 * 1024 * 1024
LANES = 128

LOG2_E = 1.4426950408889634
Q_SCALE = HEAD_DIM ** -0.5 * LOG2_E
VT_ROWS = HEAD_DIM + 16
ATT_MIN_SUM = 2.0 ** -80

PROJ_ROWS = 512
ATT_Q_ROWS = 1024
ATT_K_ROWS = 2048
ATT_SUB_K = 256
ATT_SUB_Q = 1024
ATT_SAFE_ROWS = 512
HG_ROWS = 512
HG_MAX_SPAN = 80.0
MIX_ROWS = 512
MOE_ROWS = 1024
MOE_GROUP = 4


def _dot(a, b):
    return jnp.dot(a, b, preferred_element_type=F32)


def _dot_nt(a, b):
    return lax.dot_general(a, b, (((1,), (1,)), ((), ())), preferred_element_type=F32)


def _split_bf16(x):
    hi = x.astype(BF16)
    lo = (x - hi.astype(F32)).astype(BF16)
    return hi, lo


def _sigmoid(x):
    return 1.0 / (1.0 + jnp.exp(-x))


def _group_mean_sq(x, ones_blockdiag):
    hi, lo = _split_bf16(x * x)
    return (_dot(hi, ones_blockdiag) + _dot(lo, ones_blockdiag)) * (1.0 / HEAD_DIM)


def _layer_norm(h, w, b):
    mu = jnp.mean(h, axis=-1, keepdims=True)
    hc = h - mu
    var = jnp.mean(hc * hc, axis=-1, keepdims=True)
    return hc * lax.rsqrt(var + LN_EPS) * w + b


def _rope(x, cos, sin_lo, sin_hi):
    n = x.shape[1]
    reps = n // LANES
    cos = jnp.concatenate([cos] * reps, axis=1)
    sin_lo = jnp.concatenate([sin_lo] * reps, axis=1)
    sin_hi = jnp.concatenate([sin_hi] * reps, axis=1)
    return (x * cos + pltpu.roll(x, n - ROPE_HALF, 1) * sin_lo
            + pltpu.roll(x, ROPE_HALF, 1) * sin_hi)


def _proj_kernel(x_ref, w_ref, cos_ref, sinlo_ref, sinhi_ref, qw_ref, kw_ref, lb_ref, ones_ref,
                 qt_ref, k_ref, vt_ref, hq_ref, lff_ref, kf_ref, lfb_ref, kb_ref, hv_ref, hg_ref):
    xb = x_ref[...].astype(BF16)

    def seg(off, width):
        return _dot(xb, w_ref[:, off:off + width])

    cos, sin_lo, sin_hi = cos_ref[...], sinlo_ref[...], sinhi_ref[...]

    q = seg(_OFF_Q, ATT_WIDTH)
    q = q * lax.rsqrt(_group_mean_sq(q, ones_ref[...]) + RMS_EPS) * qw_ref[...]
    q = _rope(q, cos, sin_lo, sin_hi) * Q_SCALE
    qt_ref[...] = q.T.astype(BF16)

    k = seg(_OFF_K, KV_WIDTH)
    k = k * lax.rsqrt(_group_mean_sq(k, ones_ref[0:KV_WIDTH, 0:KV_WIDTH]) + RMS_EPS) * kw_ref[...]
    k_ref[...] = _rope(k, cos, sin_lo, sin_hi).astype(BF16)

    vt = seg(_OFF_V, KV_WIDTH).T
    ones_rows = jnp.ones((VT_ROWS - HEAD_DIM, vt.shape[1]), F32)
    vt_ref[...] = jnp.concatenate(
        [piece for kv in range(N_KV_HEADS) for piece in (vt[kv * HEAD_DIM:(kv + 1) * HEAD_DIM, :], ones_rows)],
        axis=0).astype(BF16)

    hq = seg(_OFF_HQ, HG_WIDTH)
    hq_ref[...] = (hq * _sigmoid(hq)).astype(BF16)

    for off, row, lf_ref, kk_ref in ((_OFF_HFF, 0, lff_ref, kf_ref), (_OFF_HFB, 1, lfb_ref, kb_ref)):
        z = seg(off, HG_WIDTH)
        lb = lb_ref[row:row + 1, :]
        lf_ref[...] = jnp.log(lb + (1.0 - lb) * _sigmoid(z))
        kk_ref[...] = ((1.0 - lb) * _sigmoid(-z)).astype(BF16)

    hv_ref[...] = seg(_OFF_HI, HG_WIDTH).astype(BF16)
    hg = seg(_OFF_HG, HG_WIDTH)
    hg_ref[...] = (hg * _sigmoid(hg)).astype(BF16)


def _rope_tables(seq):
    rows = seq // GRID_W
    inv_freq = ROPE_THETA ** (-jnp.arange(0, ROPE_AXIS_DIM, 2, dtype=F32) / ROPE_AXIS_DIM)
    ang_r = jnp.arange(rows, dtype=F32)[:, None] * inv_freq
    ang_c = jnp.arange(GRID_W, dtype=F32)[:, None] * inv_freq
    lower = (jnp.arange(LANES) % ROPE_AXIS_DIM) < ROPE_HALF

    def table(fn):
        by_row = jnp.broadcast_to(fn(ang_r)[:, None, :], (rows, GRID_W, ROPE_HALF))
        by_col = jnp.broadcast_to(fn(ang_c)[None, :, :], (rows, GRID_W, ROPE_HALF))
        head = jnp.concatenate([by_row, by_row, by_col, by_col], axis=-1).reshape(seq, HEAD_DIM)
        return jnp.concatenate([head] * (LANES // HEAD_DIM), axis=-1)

    cos, sin = table(jnp.cos), table(jnp.sin)
    return cos, jnp.where(lower, -sin, 0.0), jnp.where(lower, 0.0, sin)


def _project(x2, batch, seq, w_in, q_norm_w, k_norm_w, lb, ones_blockdiag):
    t = batch * seq
    tm = PROJ_ROWS
    nt = seq // tm
    cos, sin_lo, sin_hi = _rope_tables(seq)
    qw = jnp.tile(q_norm_w, N_Q_HEADS)[None, :]
    kw = jnp.tile(k_norm_w, N_KV_HEADS)[None, :]

    row_block = lambda w: pl.BlockSpec((tm, w), lambda i: (i, 0))
    pos_block = pl.BlockSpec((tm, LANES), lambda i: (i % nt, 0))
    full = lambda a: pl.BlockSpec(a.shape, lambda i: (0,) * a.ndim)
    t_block = lambda w: pl.BlockSpec((None, w, tm), lambda i: (i // nt, 0, i % nt))

    hg_bf16 = jax.ShapeDtypeStruct((t, HG_WIDTH), BF16)
    hg_f32 = jax.ShapeDtypeStruct((t, HG_WIDTH), F32)
    return pl.pallas_call(
        _proj_kernel,
        grid=(t // tm,),
        in_specs=[row_block(D_MODEL), full(w_in), pos_block, pos_block, pos_block,
                  full(qw), full(kw), full(lb), full(ones_blockdiag)],
        out_specs=[t_block(ATT_WIDTH), row_block(KV_WIDTH), t_block(N_KV_HEADS * VT_ROWS),
                   row_block(HG_WIDTH), row_block(HG_WIDTH), row_block(HG_WIDTH),
                   row_block(HG_WIDTH), row_block(HG_WIDTH), row_block(HG_WIDTH), row_block(HG_WIDTH)],
        out_shape=[jax.ShapeDtypeStruct((batch, ATT_WIDTH, seq), BF16),
                   jax.ShapeDtypeStruct((t, KV_WIDTH), BF16),
                   jax.ShapeDtypeStruct((batch, N_KV_HEADS * VT_ROWS, seq), BF16),
                   hg_bf16, hg_f32, hg_bf16, hg_f32, hg_bf16, hg_bf16, hg_bf16],
        compiler_params=pltpu.CompilerParams(dimension_semantics=("parallel",),
                                             vmem_limit_bytes=VMEM_LIMIT_BYTES),
        name="proj",
    )(x2, w_in, cos, sin_lo, sin_hi, qw, kw, lb, ones_blockdiag)


def _load_queries(qt_ref, qz_ref):
    tq = qt_ref.shape[1]
    zeros = jnp.zeros((HEAD_DIM, tq), BF16)
    for h in range(N_Q_HEADS):
        qh = qt_ref[h * HEAD_DIM:(h + 1) * HEAD_DIM, :]
        qz_ref[h] = jnp.concatenate([qh, zeros] if h < Q_PER_KV else [zeros, qh], axis=0)


def _normalise_heads(numer, denom, nw_ref):
    outs = []
    for h in range(N_Q_HEADS):
        o = numer(h) / denom(h)
        ms = jnp.mean(o * o, axis=0, keepdims=True)
        outs.append(o * lax.rsqrt(ms + RMS_EPS) * nw_ref[h * HEAD_DIM:(h + 1) * HEAD_DIM, :])
    return jnp.concatenate(outs, axis=0).T


def _attn_kernel(kmax_ref, qt_ref, k_ref, vt_ref, nw_ref, o_ref, l_ref, qz_ref, c_ref, acc_ref):
    ki = pl.program_id(2)
    tq = qt_ref.shape[1]
    tk = k_ref.shape[0]

    @pl.when(ki == 0)
    def _init():
        _load_queries(qt_ref, qz_ref)
        for h in range(N_Q_HEADS):
            qf = qt_ref[h * HEAD_DIM:(h + 1) * HEAD_DIM, :].astype(F32)
            c_ref[h] = jnp.sqrt(jnp.sum(qf * qf, axis=0, keepdims=True)) * kmax_ref[...]
        acc_ref[...] = jnp.zeros(acc_ref.shape, F32)

    def finish_tile(tile):
        s, h, keys, qsl = tile
        kv = h // Q_PER_KV
        p = jnp.exp2(s - c_ref[h, :, qsl]).astype(BF16)
        acc_ref[h, :, qsl] += _dot(vt_ref[kv * VT_ROWS:(kv + 1) * VT_ROWS, keys], p)

    pending = None
    for h in range(N_Q_HEADS):
        for js in range(tk // ATT_SUB_K):
            keys = slice(js * ATT_SUB_K, (js + 1) * ATT_SUB_K)
            for qs in range(tq // ATT_SUB_Q):
                qsl = slice(qs * ATT_SUB_Q, (qs + 1) * ATT_SUB_Q)
                s = _dot(k_ref[keys, :], qz_ref[h, :, qsl])
                if pending is not None:
                    finish_tile(pending)
                pending = (s, h, keys, qsl)
    finish_tile(pending)

    @pl.when(ki == pl.num_programs(2) - 1)
    def _finish():
        o_ref[...] = _normalise_heads(lambda h: acc_ref[h, 0:HEAD_DIM, :],
                                      lambda h: acc_ref[h, HEAD_DIM:HEAD_DIM + 1, :], nw_ref)
        for h in range(N_Q_HEADS):
            l_ref[h:h + 1, :] = acc_ref[h, HEAD_DIM:HEAD_DIM + 1, :]


def _attn_safe_kernel(qt_ref, k_ref, vt_ref, nw_ref, o_ref, qz_ref, m_ref, l_ref, acc_ref):
    ki = pl.program_id(2)

    @pl.when(ki == 0)
    def _init():
        _load_queries(qt_ref, qz_ref)
        m_ref[...] = jnp.full(m_ref.shape, -jnp.inf, F32)
        l_ref[...] = jnp.zeros(l_ref.shape, F32)
        acc_ref[...] = jnp.zeros(acc_ref.shape, F32)

    k = k_ref[...]
    for h in range(N_Q_HEADS):
        kv = h // Q_PER_KV
        s = _dot(k, qz_ref[h])
        m_old = m_ref[h]
        m_new = jnp.maximum(m_old, jnp.max(s, axis=0, keepdims=True))
        alpha = jnp.exp2(m_old - m_new)
        p = jnp.exp2(s - m_new)
        l_ref[h] = alpha * l_ref[h] + jnp.sum(p, axis=0, keepdims=True)
        vt = vt_ref[kv * VT_ROWS:kv * VT_ROWS + HEAD_DIM, :]
        acc_ref[h] = alpha * acc_ref[h] + _dot(vt, p.astype(BF16))
        m_ref[h] = m_new

    @pl.when(ki == pl.num_programs(2) - 1)
    def _finish():
        o_ref[...] = _normalise_heads(lambda h: acc_ref[h], lambda h: l_ref[h], nw_ref)


def _attention(qt, k, vt, out_norm_w, k_norm_w, batch, seq):
    nw = out_norm_w[:, None]
    params = pltpu.CompilerParams(dimension_semantics=("parallel", "parallel", "arbitrary"),
                                  vmem_limit_bytes=VMEM_LIMIT_BYTES)
    out_shape = jax.ShapeDtypeStruct((batch * seq, ATT_WIDTH), F32)

    def specs(tq, tk):
        nq, nk = seq // tq, seq // tk
        return ([pl.BlockSpec((None, ATT_WIDTH, tq), lambda b, qi, ki: (b, 0, qi)),
                 pl.BlockSpec((tk, KV_WIDTH), lambda b, qi, ki: (b * nk + ki, 0)),
                 pl.BlockSpec((None, N_KV_HEADS * VT_ROWS, tk), lambda b, qi, ki: (b, 0, ki)),
                 pl.BlockSpec(nw.shape, lambda b, qi, ki: (0, 0))],
                pl.BlockSpec((tq, ATT_WIDTH), lambda b, qi, ki: (b * nq + qi, 0)), (batch, nq, nk))

    kmax = (HEAD_DIM ** 0.5 * 1.02) * jnp.max(jnp.abs(k_norm_w)).reshape(1, 1)
    tq, tk = ATT_Q_ROWS, ATT_K_ROWS
    nq = seq // tq
    in_specs, out_spec, grid = specs(tq, tk)
    att, l = pl.pallas_call(
        _attn_kernel,
        grid=grid,
        in_specs=[pl.BlockSpec((1, 1), lambda b, qi, ki: (0, 0))] + in_specs,
        out_specs=[out_spec, pl.BlockSpec((None, N_Q_HEADS, tq), lambda b, qi, ki: (b * nq + qi, 0, 0))],
        out_shape=[out_shape, jax.ShapeDtypeStruct((batch * nq, N_Q_HEADS, tq), F32)],
        scratch_shapes=[pltpu.VMEM((N_Q_HEADS, KV_WIDTH, tq), BF16),
                        pltpu.VMEM((N_Q_HEADS, 1, tq), F32),
                        pltpu.VMEM((N_Q_HEADS, VT_ROWS, tq), F32)],
        compiler_params=params,
        name="attention",
    )(kmax, qt, k, vt, nw)

    def safe():
        rows = ATT_SAFE_ROWS
        in_specs, out_spec, grid = specs(rows, rows)
        return pl.pallas_call(
            _attn_safe_kernel,
            grid=grid,
            in_specs=in_specs,
            out_specs=out_spec,
            out_shape=out_shape,
            scratch_shapes=[pltpu.VMEM((N_Q_HEADS, KV_WIDTH, rows), BF16),
                            pltpu.VMEM((N_Q_HEADS, 1, rows), F32),
                            pltpu.VMEM((N_Q_HEADS, 1, rows), F32),
                            pltpu.VMEM((N_Q_HEADS, HEAD_DIM, rows), F32)],
            compiler_params=params,
            name="attention_safe",
        )(qt, k, vt, nw)

    return lax.cond(jnp.min(l) >= ATT_MIN_SUM, lambda: att, safe)


def _hgrn_direct_scores(qf, kf, vf, b, reverse, ones_ref, rows_ref):
    n = HG_BLOCK
    b_ref, k_ref, v_ref = rows_ref
    b_ref[...] = b
    k_ref[...] = kf
    v_ref[...] = vf
    row = lax.broadcasted_iota(jnp.int32, (n, HG_WIDTH), 0)
    ones = ones_ref[...]

    def body(j, out):
        live = (row <= j) if reverse else (row >= j)
        decay = jnp.exp(jnp.where(live, b - b_ref[pl.ds(j, 1), :], -jnp.inf))
        hi, lo = _split_bf16(qf * k_ref[pl.ds(j, 1), :] * decay)
        return out + (_dot(hi, ones) + _dot(lo, ones)) * v_ref[pl.ds(j, 1), :]

    return lax.fori_loop(0, n, body, jnp.zeros((n, HG_WIDTH), F32))


def _hgrn_block(qh, lf, kk, v, reverse, direct=None):
    n = HG_BLOCK
    row = lax.broadcasted_iota(jnp.int32, (n, n), 0)
    col = lax.broadcasted_iota(jnp.int32, (n, n), 1)
    tri = (col >= row) if reverse else (col <= row)
    tri_bf16 = jnp.where(tri, 1.0, 0.0).astype(BF16)
    first_head = col < HG_DIM
    same_head = (row < HG_DIM) == first_head

    lf_hi, lf_lo = _split_bf16(lf)
    b = _dot(tri_bf16, lf_hi) + _dot(tri_bf16, lf_lo)
    if reverse:
        b_end, b_mid = b[0:1, :], b[n // 2:n // 2 + 1, :]
    else:
        b_end, b_mid = b[n - 1:n, :], b[n // 2 - 1:n // 2, :]

    yield

    qf = qh.astype(F32)
    kf = kk.astype(F32)
    q_st = (qf * jnp.exp(b)).astype(BF16)
    k_st = (kf * jnp.exp(b_end - b)).astype(BF16)
    decay = jnp.exp(b_end)
    pairs = [slice(p * n, (p + 1) * n) for p in range(HG_PAIRS)]
    zero = jnp.zeros((), BF16)
    updates = [_dot(v[:, sl].astype(F32).T.astype(BF16), k_st[:, sl]) for sl in pairs]
    if direct is None:
        q_in = (qf * jnp.exp(b - b_mid)).astype(BF16)
        k_in = (kf * jnp.exp(b_mid - b)).astype(BF16)
        span = jnp.abs(b - b_mid)
        span = jnp.max(jnp.max(span, axis=0, keepdims=True), axis=1, keepdims=True)
        scores = [(_dot_nt(jnp.where(first_head, q_in[:, sl], zero), k_in[:, sl]),
                   _dot_nt(jnp.where(first_head, zero, q_in[:, sl]), k_in[:, sl])) for sl in pairs]
        yield
        within = []
        for sl, (s0, s1) in zip(pairs, scores):
            a0 = jnp.where(tri, s0, 0.0).astype(BF16)
            a1 = jnp.where(tri, s1, 0.0).astype(BF16)
            within.append(_dot(a0, jnp.where(first_head, v[:, sl], zero))
                          + _dot(a1, jnp.where(first_head, zero, v[:, sl])))
    else:
        yield
        direct_out = _hgrn_direct_scores(qf, kf, v.astype(F32), b, reverse, *direct)
        within = [direct_out[:, sl] for sl in pairs]
        span = jnp.zeros((1, 1), F32)

    states = yield
    outs = [w + _dot_nt(q_st[:, sl], st.astype(BF16)) for w, sl, st in zip(within, pairs, states)]
    new_states = [st * decay[:, sl] + jnp.where(same_head, upd, 0.0)
                  for st, sl, upd in zip(states, pairs, updates)]
    yield jnp.concatenate(outs, axis=1), new_states, span


def _hgrn_scan(refs, of_ref, ob_ref, stf_ref, stb_ref, direct):
    qf_ref, lff_ref, kf_ref, vf_ref, qb_ref, lfb_ref, kb_ref, vb_ref = refs

    @pl.when(pl.program_id(1) == 0)
    def _init():
        stf_ref[...] = jnp.zeros(stf_ref.shape, F32)
        stb_ref[...] = jnp.zeros(stb_ref.shape, F32)

    nblk = qf_ref.shape[0] // HG_BLOCK
    blocks = []
    for j in range(nblk):
        for (q_ref, lf_ref, k_ref, v_ref, o_ref, st_ref), jj, reverse in (
                ((qf_ref, lff_ref, kf_ref, vf_ref, of_ref, stf_ref), j, False),
                ((qb_ref, lfb_ref, kb_ref, vb_ref, ob_ref, stb_ref), nblk - 1 - j, True)):
            rows = slice(jj * HG_BLOCK, (jj + 1) * HG_BLOCK)
            gen = _hgrn_block(q_ref[rows, :], lf_ref[rows, :], k_ref[rows, :], v_ref[rows, :], reverse, direct)
            blocks.append((gen, o_ref, st_ref, rows))

    for _stage in range(3):
        for block in blocks:
            next(block[0])
    spans = []
    for gen, o_ref, st_ref, rows in blocks:
        o_ref[rows, :], states, span = gen.send([st_ref[p] for p in range(HG_PAIRS)])
        for p in range(HG_PAIRS):
            st_ref[p] = states[p]
        spans.append(span)
    return functools.reduce(jnp.maximum, spans)


def _hgrn_kernel(*refs):
    *ins, of_ref, ob_ref, span_ref, stf_ref, stb_ref = refs
    span = _hgrn_scan(ins, of_ref, ob_ref, stf_ref, stb_ref, None)
    span_ref[...] = jnp.broadcast_to(span, span_ref.shape)


def _hgrn_safe_kernel(*refs):
    ones_ref, *ins, of_ref, ob_ref, stf_ref, stb_ref, b_ref, k_ref, v_ref = refs
    _hgrn_scan(ins, of_ref, ob_ref, stf_ref, stb_ref, (ones_ref, (b_ref, k_ref, v_ref)))


def _hgrn(hq, lff, kf, lfb, kb, hv, ones_blockdiag, batch, seq):
    tc = HG_ROWS
    nt = seq // tc
    fwd = pl.BlockSpec((tc, HG_WIDTH), lambda b, n: (b * nt + n, 0))
    bwd = pl.BlockSpec((tc, HG_WIDTH), lambda b, n: (b * nt + nt - 1 - n, 0))
    out = jax.ShapeDtypeStruct((batch * seq, HG_WIDTH), F32)
    state = pltpu.VMEM((HG_PAIRS, HG_BLOCK, HG_BLOCK), F32)
    params = pltpu.CompilerParams(dimension_semantics=("parallel", "arbitrary"),
                                  vmem_limit_bytes=VMEM_LIMIT_BYTES)
    operands = (hq, lff, kf, hv, hq, lfb, kb, hv)
    o_f, o_b, span = pl.pallas_call(
        _hgrn_kernel,
        grid=(batch, nt),
        in_specs=[fwd, fwd, fwd, fwd, bwd, bwd, bwd, bwd],
        out_specs=[fwd, bwd, pl.BlockSpec((None, 8, LANES), lambda b, n: (b * nt + n, 0, 0))],
        out_shape=[out, out, jax.ShapeDtypeStruct((batch * nt, 8, LANES), F32)],
        scratch_shapes=[state, state],
        compiler_params=params,
        name="hgrn",
    )(*operands)

    def safe():
        rows = pltpu.VMEM((HG_BLOCK, HG_WIDTH), F32)
        return tuple(pl.pallas_call(
            _hgrn_safe_kernel,
            grid=(batch, nt),
            in_specs=[pl.BlockSpec(ones_blockdiag.shape, lambda b, n: (0, 0)),
                      fwd, fwd, fwd, fwd, bwd, bwd, bwd, bwd],
            out_specs=[fwd, bwd],
            out_shape=[out, out],
            scratch_shapes=[state, state, rows, rows, rows],
            compiler_params=params,
            name="hgrn_safe",
        )(ones_blockdiag, *operands))

    return lax.cond(jnp.max(span) < HG_MAX_SPAN, lambda: (o_f, o_b), safe)


def _route(scores, biased):
    tm = scores.shape[1]
    neg_inf = -jnp.inf
    group_scores = []
    for g in range(N_GROUPS):
        blk = biased[g * EXPERTS_PER_GROUP:(g + 1) * EXPERTS_PER_GROUP, :]
        m1 = jnp.max(blk, axis=0, keepdims=True)
        n_max = jnp.sum(jnp.where(blk == m1, 1.0, 0.0), axis=0, keepdims=True)
        below = jnp.max(jnp.where(blk < m1, blk, neg_inf), axis=0, keepdims=True)
        group_scores.append(m1 + jnp.where(n_max >= 2.0, m1, below))
    gs = jnp.concatenate(group_scores, axis=0)

    gidx = lax.broadcasted_iota(jnp.int32, (N_GROUPS, tm), 0)
    rank = jnp.zeros((N_GROUPS, tm), F32)
    for g in range(N_GROUPS):
        other = gs[g:g + 1, :]
        ahead = jnp.where(other > gs, 1.0, jnp.where(other == gs, jnp.where(gidx > g, 1.0, 0.0), 0.0))
        rank = rank + ahead
    group_on = jnp.where(rank < float(TOPK_GROUPS), 1.0, 0.0)
    expert_on = jnp.concatenate(
        [jnp.broadcast_to(group_on[g:g + 1, :], (EXPERTS_PER_GROUP, tm)) for g in range(N_GROUPS)], axis=0)

    eidx = lax.broadcasted_iota(jnp.int32, (N_EXPERTS, tm), 0).astype(F32)
    work = jnp.where(expert_on > 0.0, biased, neg_inf)
    chosen = jnp.zeros((N_EXPERTS, tm), F32)
    for _ in range(TOP_K):
        best = jnp.max(work, axis=0, keepdims=True)
        first = jnp.min(jnp.where(work == best, eidx, float(N_EXPERTS)), axis=0, keepdims=True)
        pick = eidx == first
        chosen = jnp.where(pick, 1.0, chosen)
        work = jnp.where(pick, neg_inf, work)
    w = jnp.where(chosen > 0.0, scores, 0.0)
    return w / jnp.sum(w, axis=0, keepdims=True) * ROUTED_SCALE


def _mix_kernel(x_ref, att_ref, of_ref, ob_ref, g_ref, wout_ref, hnw_ref, lnw_ref, lnb_ref,
                wrhi_ref, wrlo_ref, rb_ref, ones_ref, x1_ref, gate_ref):
    o = of_ref[...] + ob_ref[...]
    o = o * lax.rsqrt(_group_mean_sq(o, ones_ref[...]) + RMS_EPS) * hnw_ref[...] * g_ref[...].astype(F32)
    mixed = jnp.concatenate([att_ref[...].astype(BF16), o.astype(BF16)], axis=1)
    h = DEEPNORM_ALPHA * x_ref[...] + _dot(mixed, wout_ref[...])
    x1 = _layer_norm(h, lnw_ref[...], lnb_ref[...])
    x1_ref[...] = x1

    x_hi, x_lo = _split_bf16(x1)
    w_hi, w_lo = wrhi_ref[...], wrlo_ref[...]
    logits = _dot_nt(w_hi, x_hi) + _dot_nt(w_hi, x_lo) + _dot_nt(w_lo, x_hi)
    scores = _sigmoid(logits)
    gate = _route(scores, scores + rb_ref[...])
    gate_ref[...] = jnp.concatenate([gate, jnp.zeros_like(gate)], axis=0).T


def _mix(x2, att, o_f, o_b, hg, w_out, hg_norm_w, ln_w, ln_b, w_router, router_bias, ones_blockdiag):
    t = x2.shape[0]
    tm = MIX_ROWS
    wr_hi, wr_lo = _split_bf16(w_router.T)
    args = (x2, att, o_f, o_b, hg, w_out, hg_norm_w[None, :], ln_w[None, :], ln_b[None, :],
            wr_hi, wr_lo, router_bias[:, None], ones_blockdiag)
    row_block = lambda w: pl.BlockSpec((tm, w), lambda i: (i, 0))
    full = lambda a: pl.BlockSpec(a.shape, lambda i: (0,) * a.ndim)
    return pl.pallas_call(
        _mix_kernel,
        grid=(t // tm,),
        in_specs=[row_block(D_MODEL), row_block(ATT_WIDTH), row_block(HG_WIDTH), row_block(HG_WIDTH),
                  row_block(HG_WIDTH)] + [full(a) for a in args[5:]],
        out_specs=[row_block(D_MODEL), row_block(LANES)],
        out_shape=[jax.ShapeDtypeStruct((t, D_MODEL), F32), jax.ShapeDtypeStruct((t, LANES), F32)],
        compiler_params=pltpu.CompilerParams(dimension_semantics=("parallel",),
                                             vmem_limit_bytes=VMEM_LIMIT_BYTES),
        name="mix",
    )(*args)


def _swiglu(xb, w_gate, w_up):
    hg = _dot(xb, w_gate)
    return hg * _sigmoid(hg) * _dot(xb, w_up)


def _moe_kernel(x1_ref, gate_ref, wg_ref, wu_ref, wd_ref, winv_ref, wsg_ref, wsu_ref, wsd_ref, lnw_ref, lnb_ref,
                o_ref, x8_ref, xinv_ref, acc_ref):
    step = pl.program_id(1)

    @pl.when(step == 0)
    def _init():
        x1 = x1_ref[...]
        peak = jnp.maximum(jnp.max(jnp.abs(x1), axis=1, keepdims=True), FP8_TINY)
        x8_ref[...] = (x1 * (FP8_PEAK / peak)).astype(FP8)
        xinv_ref[...] = peak * (1.0 / FP8_PEAK)
        xb = x1.astype(BF16)
        acc_ref[...] = _dot(_swiglu(xb, wsg_ref[...], wsu_ref[...]).astype(BF16), wsd_ref[...])

    x8 = x8_ref[...]
    xinv = xinv_ref[...]
    lane = lax.broadcasted_iota(jnp.int32, gate_ref.shape, 1)
    hidden = []
    for j in range(MOE_GROUP):
        raw_gate = _dot(x8, wg_ref[j])
        g = jnp.sum(jnp.where(lane == step * MOE_GROUP + j, gate_ref[...], 0.0), axis=1, keepdims=True)
        gate_scale = xinv * winv_ref[j, 0:1, 0:1]
        sig = 1.0 / (1.0 + jnp.exp2(raw_gate * (gate_scale * -LOG2_E)))
        row_scale = gate_scale * g * xinv * (winv_ref[j, 0:1, 1:2] * winv_ref[j, 0:1, 2:3])
        hidden.append(raw_gate * sig * _dot(x8, wu_ref[j]) * row_scale)
    hidden = jnp.concatenate(hidden, axis=1)
    peak = jnp.maximum(jnp.max(jnp.abs(hidden), axis=1, keepdims=True), FP8_TINY)
    h8 = (hidden * (FP8_PEAK / peak)).astype(FP8)
    w_down = wd_ref[...].reshape(MOE_GROUP * EXPERT_HIDDEN, D_MODEL)
    acc_ref[...] += _dot(h8, w_down) * (peak * (1.0 / FP8_PEAK))

    @pl.when(step == pl.num_programs(1) - 1)
    def _finish():
        h = DEEPNORM_ALPHA * x1_ref[...] + acc_ref[...]
        o_ref[...] = _layer_norm(h, lnw_ref[...], lnb_ref[...])


def _quantize_kernel(wg_ref, wu_ref, wd_ref, wg8_ref, wu8_ref, wd8_ref, inv_ref):
    lane = lax.broadcasted_iota(jnp.int32, inv_ref.shape, 1)
    inv = jnp.zeros(inv_ref.shape, F32)
    for i, (w_ref, w8_ref) in enumerate(((wg_ref, wg8_ref), (wu_ref, wu8_ref), (wd_ref, wd8_ref))):
        w = w_ref[...]
        peak = jnp.max(jnp.max(jnp.abs(w), axis=0, keepdims=True), axis=1, keepdims=True)
        peak = jnp.maximum(peak, FP8_TINY)
        w8_ref[...] = (w * (FP8_PEAK / peak)).astype(FP8)
        inv = jnp.where(lane == i, peak * (1.0 / FP8_PEAK), inv)
    inv_ref[...] = inv


def _quantize_experts(w_gate, w_up, w_down):
    spec = lambda a: pl.BlockSpec((None,) + a.shape[1:], lambda e: (e, 0, 0))
    fp8 = lambda a: jax.ShapeDtypeStruct(a.shape, FP8)
    inv = jax.ShapeDtypeStruct((N_EXPERTS, 8, LANES), F32)
    return pl.pallas_call(
        _quantize_kernel,
        grid=(N_EXPERTS,),
        in_specs=[spec(w_gate), spec(w_up), spec(w_down)],
        out_specs=[spec(w_gate), spec(w_up), spec(w_down), spec(inv)],
        out_shape=[fp8(w_gate), fp8(w_up), fp8(w_down), inv],
        compiler_params=pltpu.CompilerParams(dimension_semantics=("parallel",),
                                             vmem_limit_bytes=VMEM_LIMIT_BYTES),
        name="quantize",
    )(w_gate, w_up, w_down)


def _moe(x1, gate, w_eg8, w_eu8, w_ed, w_inv, w_sg, w_su, w_sd, ln_w, ln_b):
    t = x1.shape[0]
    tm = MOE_ROWS
    args = (x1, gate, w_eg8, w_eu8, w_ed, w_inv, w_sg, w_su, w_sd, ln_w[None, :], ln_b[None, :])
    row_block = lambda w: pl.BlockSpec((tm, w), lambda i, g: (i, 0))
    full = lambda a: pl.BlockSpec(a.shape, lambda i, g: (0,) * a.ndim)
    experts = lambda a: pl.BlockSpec((MOE_GROUP,) + a.shape[1:], lambda i, g: (g, 0, 0))
    return pl.pallas_call(
        _moe_kernel,
        grid=(t // tm, N_EXPERTS // MOE_GROUP),
        in_specs=[row_block(D_MODEL), row_block(LANES), experts(w_eg8), experts(w_eu8), experts(w_ed),
                  experts(w_inv)] + [full(a) for a in args[6:]],
        out_specs=row_block(D_MODEL),
        out_shape=jax.ShapeDtypeStruct((t, D_MODEL), F32),
        scratch_shapes=[pltpu.VMEM((tm, D_MODEL), FP8), pltpu.VMEM((tm, 1), F32),
                        pltpu.VMEM((tm, D_MODEL), F32)],
        compiler_params=pltpu.CompilerParams(dimension_semantics=("parallel", "arbitrary"),
                                             vmem_limit_bytes=VMEM_LIMIT_BYTES),
        name="moe",
    )(*args)


def _trunk(x, p):
    batch, seq, _ = x.shape
    for rows in (PROJ_ROWS, ATT_Q_ROWS, ATT_K_ROWS, ATT_SAFE_ROWS, HG_ROWS, GRID_W):
        assert seq % rows == 0, (seq, rows)
    for rows in (MIX_ROWS, MOE_ROWS):
        assert (batch * seq) % rows == 0, (batch, seq, rows)
    x2 = x.reshape(batch * seq, D_MODEL)
    qt, k, vt, hq, lff, kf, lfb, kb, hv, hg = _project(
        x2, batch, seq, p["w_in"], p["q_norm_w"], p["k_norm_w"], p["lb"], p["ones"])
    att = _attention(qt, k, vt, p["attn_out_norm_w"], p["k_norm_w"], batch, seq)
    o_f, o_b = _hgrn(hq, lff, kf, lfb, kb, hv, p["ones"], batch, seq)
    x1, gate = _mix(x2, att, o_f, o_b, hg, p["w_out"], p["hgrn_out_norm_w"], p["ln1_w"], p["ln1_b"],
                    p["w_router"], p["router_bias"], p["ones"])
    y = _moe(x1, gate, p["w_exp_gate"], p["w_exp_up"], p["w_exp_down"], p["w_exp_inv"], p["w_sh_gate"],
             p["w_sh_up"], p["w_sh_down"], p["ln2_w"], p["ln2_b"])
    return y.reshape(batch, seq, D_MODEL)


def _prepare(w_in, q_norm_w, k_norm_w, attn_out_norm_w, hgrn_lb_logits, hgrn_out_norm_w, w_out, ln1_w, ln1_b, w_router, router_bias, w_exp_gate, w_exp_up, w_exp_down, w_sh_gate, w_sh_up, w_sh_down, ln2_w, ln2_b):
    layer = 0
    lb_all = jnp.cumsum(jax.nn.softmax(hgrn_lb_logits.astype(F32), axis=0), axis=0)
    head_of = jnp.arange(ATT_WIDTH) // HEAD_DIM
    w_eg8, w_eu8, w_ed8, w_inv = _quantize_experts(w_exp_gate[layer], w_exp_up[layer], w_exp_down[layer])
    return {
        "w_in": w_in[layer].astype(BF16),
        "q_norm_w": q_norm_w[layer], "k_norm_w": k_norm_w[layer],
        "attn_out_norm_w": attn_out_norm_w[layer],
        "lb": lb_all[layer],
        "hgrn_out_norm_w": hgrn_out_norm_w[layer],
        "w_out": w_out[layer].astype(BF16),
        "ln1_w": ln1_w[layer], "ln1_b": ln1_b[layer],
        "w_router": w_router[layer], "router_bias": router_bias[layer],
        "w_exp_gate": w_eg8, "w_exp_up": w_eu8, "w_exp_down": w_ed8, "w_exp_inv": w_inv,
        "w_sh_gate": w_sh_gate[layer].astype(BF16), "w_sh_up": w_sh_up[layer].astype(BF16),
        "w_sh_down": w_sh_down[layer].astype(BF16),
        "ln2_w": ln2_w[layer], "ln2_b": ln2_b[layer],
        "ones": (head_of[:, None] == head_of[None, :]).astype(BF16),
    }


def kernel(x_prompt, x_sample, w_in, q_norm_w, k_norm_w, attn_out_norm_w, hgrn_lb_logits, hgrn_out_norm_w, w_out, ln1_w, ln1_b, w_router, router_bias, w_exp_gate, w_exp_up, w_exp_down, w_sh_gate, w_sh_up, w_sh_down, ln2_w, ln2_b):
    p = _prepare(w_in, q_norm_w, k_norm_w, attn_out_norm_w, hgrn_lb_logits, hgrn_out_norm_w, w_out, ln1_w, ln1_b, w_router, router_bias, w_exp_gate, w_exp_up, w_exp_down, w_sh_gate, w_sh_up, w_sh_down, ln2_w, ln2_b)
    return _trunk(x_prompt, p), _trunk(x_sample, p)
```

```python
import functools

import jax
import jax.numpy as jnp
from jax import lax
from jax.experimental import pallas as pl
from jax.experimental.pallas import tpu as pltpu

F32 = jnp.float32
BF16 = jnp.bfloat16
FP8 = jnp.float8_e4m3fn
FP8_PEAK = 256.0
FP8_TINY = 1e-30

D_MODEL = 1024
N_Q_HEADS = 8
N_KV_HEADS = 2
HEAD_DIM = 64
Q_PER_KV = N_Q_HEADS // N_KV_HEADS
ATT_WIDTH = N_Q_HEADS * HEAD_DIM
KV_WIDTH = N_KV_HEADS * HEAD_DIM
ROPE_THETA = 10000.0
ROPE_AXIS_DIM = HEAD_DIM // 2
ROPE_HALF = ROPE_AXIS_DIM // 2
GRID_W = 64
HG_HEADS = 8
HG_DIM = 64
HG_WIDTH = HG_HEADS * HG_DIM
HG_BLOCK = 128
HG_PAIRS = HG_WIDTH // HG_BLOCK
N_EXPERTS = 64
N_GROUPS = 8
EXPERTS_PER_GROUP = N_EXPERTS // N_GROUPS
TOPK_GROUPS = 4
TOP_K = 8
EXPERT_HIDDEN = 256
ROUTED_SCALE = 2.5
DEPTH = 1
DEEPNORM_ALPHA = (2.0 * DEPTH) ** 0.25
RMS_EPS = 1e-6
LN_EPS = 1e-5

_OFF_Q = 0
_OFF_K = _OFF_Q + ATT_WIDTH
_OFF_V = _OFF_K + KV_WIDTH
_OFF_HQ = _OFF_V + KV_WIDTH
_OFF_HFF = _OFF_HQ + HG_WIDTH
_OFF_HFB = _OFF_HFF + HG_WIDTH
_OFF_HI = _OFF_HFB + HG_WIDTH
_OFF_HG = _OFF_HI + HG_WIDTH
IN_WIDTH = _OFF_HG + HG_WIDTH

VMEM_LIMIT_BYTES = 56 * 1024 * 1024
LANES = 128

LOG2_E = 1.4426950408889634
Q_SCALE = HEAD_DIM ** -0.5 * LOG2_E
VT_ROWS = HEAD_DIM + 16
ATT_MIN_SUM = 2.0 ** -80

PROJ_ROWS = 512
ATT_Q_ROWS = 1024
ATT_K_ROWS = 2048
ATT_SUB_K = 256
ATT_SUB_Q = 1024
ATT_SAFE_ROWS = 512
HG_ROWS = 512
HG_MAX_SPAN = 80.0
MIX_ROWS = 512
MOE_ROWS = 1024
MOE_GROUP = 8


def _dot(a, b):
    return jnp.dot(a, b, preferred_element_type=F32)


def _dot_nt(a, b):
    return lax.dot_general(a, b, (((1,), (1,)), ((), ())), preferred_element_type=F32)


def _split_bf16(x):
    hi = x.astype(BF16)
    lo = (x - hi.astype(F32)).astype(BF16)
    return hi, lo


def _sigmoid(x):
    return 1.0 / (1.0 + jnp.exp(-x))


def _group_mean_sq(x, ones_blockdiag):
    hi, lo = _split_bf16(x * x)
    return (_dot(hi, ones_blockdiag) + _dot(lo, ones_blockdiag)) * (1.0 / HEAD_DIM)


def _layer_norm(h, w, b):
    mu = jnp.mean(h, axis=-1, keepdims=True)
    hc = h - mu
    var = jnp.mean(hc * hc, axis=-1, keepdims=True)
    return hc * lax.rsqrt(var + LN_EPS) * w + b


def _rope(x, cos, sin_lo, sin_hi):
    n = x.shape[1]
    reps = n // LANES
    cos = jnp.concatenate([cos] * reps, axis=1)
    sin_lo = jnp.concatenate([sin_lo] * reps, axis=1)
    sin_hi = jnp.concatenate([sin_hi] * reps, axis=1)
    return (x * cos + pltpu.roll(x, n - ROPE_HALF, 1) * sin_lo
            + pltpu.roll(x, ROPE_HALF, 1) * sin_hi)


def _proj_kernel(x_ref, w_ref, cos_ref, sinlo_ref, sinhi_ref, qw_ref, kw_ref, lb_ref, ones_ref,
                 qt_ref, k_ref, vt_ref, hq_ref, lff_ref, kf_ref, lfb_ref, kb_ref, hv_ref, hg_ref):
    xb = x_ref[...].astype(BF16)

    def seg(off, width):
        return _dot(xb, w_ref[:, off:off + width])

    cos, sin_lo, sin_hi = cos_ref[...], sinlo_ref[...], sinhi_ref[...]

    q = seg(_OFF_Q, ATT_WIDTH)
    q = q * lax.rsqrt(_group_mean_sq(q, ones_ref[...]) + RMS_EPS) * qw_ref[...]
    q = _rope(q, cos, sin_lo, sin_hi) * Q_SCALE
    qt_ref[...] = q.T.astype(BF16)

    k = seg(_OFF_K, KV_WIDTH)
    k = k * lax.rsqrt(_group_mean_sq(k, ones_ref[0:KV_WIDTH, 0:KV_WIDTH]) + RMS_EPS) * kw_ref[...]
    k_ref[...] = _rope(k, cos, sin_lo, sin_hi).astype(BF16)

    vt = seg(_OFF_V, KV_WIDTH).T
    ones_rows = jnp.ones((VT_ROWS - HEAD_DIM, vt.shape[1]), F32)
    vt_ref[...] = jnp.concatenate(
        [piece for kv in range(N_KV_HEADS) for piece in (vt[kv * HEAD_DIM:(kv + 1) * HEAD_DIM, :], ones_rows)],
        axis=0).astype(BF16)

    hq = seg(_OFF_HQ, HG_WIDTH)
    hq_ref[...] = (hq * _sigmoid(hq)).astype(BF16)

    for off, row, lf_ref, kk_ref in ((_OFF_HFF, 0, lff_ref, kf_ref), (_OFF_HFB, 1, lfb_ref, kb_ref)):
        z = seg(off, HG_WIDTH)
        lb = lb_ref[row:row + 1, :]
        lf_ref[...] = jnp.log(lb + (1.0 - lb) * _sigmoid(z))
        kk_ref[...] = ((1.0 - lb) * _sigmoid(-z)).astype(BF16)

    hv_ref[...] = seg(_OFF_HI, HG_WIDTH).astype(BF16)
    hg = seg(_OFF_HG, HG_WIDTH)
    hg_ref[...] = (hg * _sigmoid(hg)).astype(BF16)


def _rope_tables(seq):
    rows = seq // GRID_W
    inv_freq = ROPE_THETA ** (-jnp.arange(0, ROPE_AXIS_DIM, 2, dtype=F32) / ROPE_AXIS_DIM)
    ang_r = jnp.arange(rows, dtype=F32)[:, None] * inv_freq
    ang_c = jnp.arange(GRID_W, dtype=F32)[:, None] * inv_freq
    lower = (jnp.arange(LANES) % ROPE_AXIS_DIM) < ROPE_HALF

    def table(fn):
        by_row = jnp.broadcast_to(fn(ang_r)[:, None, :], (rows, GRID_W, ROPE_HALF))
        by_col = jnp.broadcast_to(fn(ang_c)[None, :, :], (rows, GRID_W, ROPE_HALF))
        head = jnp.concatenate([by_row, by_row, by_col, by_col], axis=-1).reshape(seq, HEAD_DIM)
        return jnp.concatenate([head] * (LANES // HEAD_DIM), axis=-1)

    cos, sin = table(jnp.cos), table(jnp.sin)
    return cos, jnp.where(lower, -sin, 0.0), jnp.where(lower, 0.0, sin)


def _project(x2, batch, seq, w_in, q_norm_w, k_norm_w, lb, ones_blockdiag):
    t = batch * seq
    tm = PROJ_ROWS
    nt = seq // tm
    cos, sin_lo, sin_hi = _rope_tables(seq)
    qw = jnp.tile(q_norm_w, N_Q_HEADS)[None, :]
    kw = jnp.tile(k_norm_w, N_KV_HEADS)[None, :]

    row_block = lambda w: pl.BlockSpec((tm, w), lambda i: (i, 0))
    pos_block = pl.BlockSpec((tm, LANES), lambda i: (i % nt, 0))
    full = lambda a: pl.BlockSpec(a.shape, lambda i: (0,) * a.ndim)
    t_block = lambda w: pl.BlockSpec((None, w, tm), lambda i: (i // nt, 0, i % nt))

    hg_bf16 = jax.ShapeDtypeStruct((t, HG_WIDTH), BF16)
    hg_f32 = jax.ShapeDtypeStruct((t, HG_WIDTH), F32)
    return pl.pallas_call(
        _proj_kernel,
        grid=(t // tm,),
        in_specs=[row_block(D_MODEL), full(w_in), pos_block, pos_block, pos_block,
                  full(qw), full(kw), full(lb), full(ones_blockdiag)],
        out_specs=[t_block(ATT_WIDTH), row_block(KV_WIDTH), t_block(N_KV_HEADS * VT_ROWS),
                   row_block(HG_WIDTH), row_block(HG_WIDTH), row_block(HG_WIDTH),
                   row_block(HG_WIDTH), row_block(HG_WIDTH), row_block(HG_WIDTH), row_block(HG_WIDTH)],
        out_shape=[jax.ShapeDtypeStruct((batch, ATT_WIDTH, seq), BF16),
                   jax.ShapeDtypeStruct((t, KV_WIDTH), BF16),
                   jax.ShapeDtypeStruct((batch, N_KV_HEADS * VT_ROWS, seq), BF16),
                   hg_bf16, hg_f32, hg_bf16, hg_f32, hg_bf16, hg_bf16, hg_bf16],
        compiler_params=pltpu.CompilerParams(dimension_semantics=("parallel",),
                                             vmem_limit_bytes=VMEM_LIMIT_BYTES),
        name="proj",
    )(x2, w_in, cos, sin_lo, sin_hi, qw, kw, lb, ones_blockdiag)


def _load_queries(qt_ref, qz_ref):
    tq = qt_ref.shape[1]
    zeros = jnp.zeros((HEAD_DIM, tq), BF16)
    for h in range(N_Q_HEADS):
        qh = qt_ref[h * HEAD_DIM:(h + 1) * HEAD_DIM, :]
        qz_ref[h] = jnp.concatenate([qh, zeros] if h < Q_PER_KV else [zeros, qh], axis=0)


def _normalise_heads(numer, denom, nw_ref):
    outs = []
    for h in range(N_Q_HEADS):
        o = numer(h) / denom(h)
        ms = jnp.mean(o * o, axis=0, keepdims=True)
        outs.append(o * lax.rsqrt(ms + RMS_EPS) * nw_ref[h * HEAD_DIM:(h + 1) * HEAD_DIM, :])
    return jnp.concatenate(outs, axis=0).T


def _attn_kernel(kmax_ref, qt_ref, k_ref, vt_ref, nw_ref, o_ref, l_ref, qz_ref, c_ref, acc_ref):
    ki = pl.program_id(2)
    tq = qt_ref.shape[1]
    tk = k_ref.shape[0]

    @pl.when(ki == 0)
    def _init():
        _load_queries(qt_ref, qz_ref)
        for h in range(N_Q_HEADS):
            qf = qt_ref[h * HEAD_DIM:(h + 1) * HEAD_DIM, :].astype(F32)
            c_ref[h] = jnp.sqrt(jnp.sum(qf * qf, axis=0, keepdims=True)) * kmax_ref[...]
        acc_ref[...] = jnp.zeros(acc_ref.shape, F32)

    def finish_tile(tile):
        s, h, keys, qsl = tile
        kv = h // Q_PER_KV
        p = jnp.exp2(s - c_ref[h, :, qsl]).astype(BF16)
        acc_ref[h, :, qsl] += _dot(vt_ref[kv * VT_ROWS:(kv + 1) * VT_ROWS, keys], p)

    pending = None
    for h in range(N_Q_HEADS):
        for js in range(tk // ATT_SUB_K):
            keys = slice(js * ATT_SUB_K, (js + 1) * ATT_SUB_K)
            for qs in range(tq // ATT_SUB_Q):
                qsl = slice(qs * ATT_SUB_Q, (qs + 1) * ATT_SUB_Q)
                s = _dot(k_ref[keys, :], qz_ref[h, :, qsl])
                if pending is not None:
                    finish_tile(pending)
                pending = (s, h, keys, qsl)
    finish_tile(pending)

    @pl.when(ki == pl.num_programs(2) - 1)
    def _finish():
        o_ref[...] = _normalise_heads(lambda h: acc_ref[h, 0:HEAD_DIM, :],
                                      lambda h: acc_ref[h, HEAD_DIM:HEAD_DIM + 1, :], nw_ref)
        for h in range(N_Q_HEADS):
            l_ref[h:h + 1, :] = acc_ref[h, HEAD_DIM:HEAD_DIM + 1, :]


def _attn_safe_kernel(qt_ref, k_ref, vt_ref, nw_ref, o_ref, qz_ref, m_ref, l_ref, acc_ref):
    ki = pl.program_id(2)

    @pl.when(ki == 0)
    def _init():
        _load_queries(qt_ref, qz_ref)
        m_ref[...] = jnp.full(m_ref.shape, -jnp.inf, F32)
        l_ref[...] = jnp.zeros(l_ref.shape, F32)
        acc_ref[...] = jnp.zeros(acc_ref.shape, F32)

    k = k_ref[...]
    for h in range(N_Q_HEADS):
        kv = h // Q_PER_KV
        s = _dot(k, qz_ref[h])
        m_old = m_ref[h]
        m_new = jnp.maximum(m_old, jnp.max(s, axis=0, keepdims=True))
        alpha = jnp.exp2(m_old - m_new)
        p = jnp.exp2(s - m_new)
        l_ref[h] = alpha * l_ref[h] + jnp.sum(p, axis=0, keepdims=True)
        vt = vt_ref[kv * VT_ROWS:kv * VT_ROWS + HEAD_DIM, :]
        acc_ref[h] = alpha * acc_ref[h] + _dot(vt, p.astype(BF16))
        m_ref[h] = m_new

    @pl.when(ki == pl.num_programs(2) - 1)
    def _finish():
        o_ref[...] = _normalise_heads(lambda h: acc_ref[h], lambda h: l_ref[h], nw_ref)


def _attention(qt, k, vt, out_norm_w, k_norm_w, batch, seq):
    nw = out_norm_w[:, None]
    params = pltpu.CompilerParams(dimension_semantics=("parallel", "parallel", "arbitrary"),
                                  vmem_limit_bytes=VMEM_LIMIT_BYTES)
    out_shape = jax.ShapeDtypeStruct((batch * seq, ATT_WIDTH), F32)

    def specs(tq, tk):
        nq, nk = seq // tq, seq // tk
        return ([pl.BlockSpec((None, ATT_WIDTH, tq), lambda b, qi, ki: (b, 0, qi)),
                 pl.BlockSpec((tk, KV_WIDTH), lambda b, qi, ki: (b * nk + ki, 0)),
                 pl.BlockSpec((None, N_KV_HEADS * VT_ROWS, tk), lambda b, qi, ki: (b, 0, ki)),
                 pl.BlockSpec(nw.shape, lambda b, qi, ki: (0, 0))],
                pl.BlockSpec((tq, ATT_WIDTH), lambda b, qi, ki: (b * nq + qi, 0)), (batch, nq, nk))

    kmax = (HEAD_DIM ** 0.5 * 1.02) * jnp.max(jnp.abs(k_norm_w)).reshape(1, 1)
    tq, tk = ATT_Q_ROWS, ATT_K_ROWS
    nq = seq // tq
    in_specs, out_spec, grid = specs(tq, tk)
    att, l = pl.pallas_call(
        _attn_kernel,
        grid=grid,
        in_specs=[pl.BlockSpec((1, 1), lambda b, qi, ki: (0, 0))] + in_specs,
        out_specs=[out_spec, pl.BlockSpec((None, N_Q_HEADS, tq), lambda b, qi, ki: (b * nq + qi, 0, 0))],
        out_shape=[out_shape, jax.ShapeDtypeStruct((batch * nq, N_Q_HEADS, tq), F32)],
        scratch_shapes=[pltpu.VMEM((N_Q_HEADS, KV_WIDTH, tq), BF16),
                        pltpu.VMEM((N_Q_HEADS, 1, tq), F32),
                        pltpu.VMEM((N_Q_HEADS, VT_ROWS, tq), F32)],
        compiler_params=params,
        name="attention",
    )(kmax, qt, k, vt, nw)

    def safe():
        rows = ATT_SAFE_ROWS
        in_specs, out_spec, grid = specs(rows, rows)
        return pl.pallas_call(
            _attn_safe_kernel,
            grid=grid,
            in_specs=in_specs,
            out_specs=out_spec,
            out_shape=out_shape,
            scratch_shapes=[pltpu.VMEM((N_Q_HEADS, KV_WIDTH, rows), BF16),
                            pltpu.VMEM((N_Q_HEADS, 1, rows), F32),
                            pltpu.VMEM((N_Q_HEADS, 1, rows), F32),
                            pltpu.VMEM((N_Q_HEADS, HEAD_DIM, rows), F32)],
            compiler_params=params,
            name="attention_safe",
        )(qt, k, vt, nw)

    return lax.cond(jnp.min(l) >= ATT_MIN_SUM, lambda: att, safe)


def _hgrn_direct_scores(qf, kf, vf, b, reverse, ones_ref, rows_ref):
    n = HG_BLOCK
    b_ref, k_ref, v_ref = rows_ref
    b_ref[...] = b
    k_ref[...] = kf
    v_ref[...] = vf
    row = lax.broadcasted_iota(jnp.int32, (n, HG_WIDTH), 0)
    ones = ones_ref[...]

    def body(j, out):
        live = (row <= j) if reverse else (row >= j)
        decay = jnp.exp(jnp.where(live, b - b_ref[pl.ds(j, 1), :], -jnp.inf))
        hi, lo = _split_bf16(qf * k_ref[pl.ds(j, 1), :] * decay)
        return out + (_dot(hi, ones) + _dot(lo, ones)) * v_ref[pl.ds(j, 1), :]

    return lax.fori_loop(0, n, body, jnp.zeros((n, HG_WIDTH), F32))


def _hgrn_block(qh, lf, kk, v, reverse, direct=None):
    n = HG_BLOCK
    row = lax.broadcasted_iota(jnp.int32, (n, n), 0)
    col = lax.broadcasted_iota(jnp.int32, (n, n), 1)
    tri = (col >= row) if reverse else (col <= row)
    tri_bf16 = jnp.where(tri, 1.0, 0.0).astype(BF16)
    first_head = col < HG_DIM
    same_head = (row < HG_DIM) == first_head

    lf_hi, lf_lo = _split_bf16(lf)
    b = _dot(tri_bf16, lf_hi) + _dot(tri_bf16, lf_lo)
    if reverse:
        b_end, b_mid = b[0:1, :], b[n // 2:n // 2 + 1, :]
    else:
        b_end, b_mid = b[n - 1:n, :], b[n // 2 - 1:n // 2, :]

    yield

    qf = qh.astype(F32)
    kf = kk.astype(F32)
    q_st = (qf * jnp.exp(b)).astype(BF16)
    k_st = (kf * jnp.exp(b_end - b)).astype(BF16)
    decay = jnp.exp(b_end)
    pairs = [slice(p * n, (p + 1) * n) for p in range(HG_PAIRS)]
    zero = jnp.zeros((), BF16)
    updates = [_dot(v[:, sl].astype(F32).T.astype(BF16), k_st[:, sl]) for sl in pairs]
    if direct is None:
        q_in = (qf * jnp.exp(b - b_mid)).astype(BF16)
        k_in = (kf * jnp.exp(b_mid - b)).astype(BF16)
        span = jnp.abs(b - b_mid)
        span = jnp.max(jnp.max(span, axis=0, keepdims=True), axis=1, keepdims=True)
        scores = [(_dot_nt(jnp.where(first_head, q_in[:, sl], zero), k_in[:, sl]),
                   _dot_nt(jnp.where(first_head, zero, q_in[:, sl]), k_in[:, sl])) for sl in pairs]
        yield
        within = []
        for sl, (s0, s1) in zip(pairs, scores):
            a0 = jnp.where(tri, s0, 0.0).astype(BF16)
            a1 = jnp.where(tri, s1, 0.0).astype(BF16)
            within.append(_dot(a0, jnp.where(first_head, v[:, sl], zero))
                          + _dot(a1, jnp.where(first_head, zero, v[:, sl])))
    else:
        yield
        direct_out = _hgrn_direct_scores(qf, kf, v.astype(F32), b, reverse, *direct)
        within = [direct_out[:, sl] for sl in pairs]
        span = jnp.zeros((1, 1), F32)

    states = yield
    outs = [w + _dot_nt(q_st[:, sl], st.astype(BF16)) for w, sl, st in zip(within, pairs, states)]
    new_states = [st * decay[:, sl] + jnp.where(same_head, upd, 0.0)
                  for st, sl, upd in zip(states, pairs, updates)]
    yield jnp.concatenate(outs, axis=1), new_states, span


def _hgrn_scan(refs, of_ref, ob_ref, stf_ref, stb_ref, direct):
    qf_ref, lff_ref, kf_ref, vf_ref, qb_ref, lfb_ref, kb_ref, vb_ref = refs

    @pl.when(pl.program_id(1) == 0)
    def _init():
        stf_ref[...] = jnp.zeros(stf_ref.shape, F32)
        stb_ref[...] = jnp.zeros(stb_ref.shape, F32)

    nblk = qf_ref.shape[0] // HG_BLOCK
    blocks = []
    for j in range(nblk):
        for (q_ref, lf_ref, k_ref, v_ref, o_ref, st_ref), jj, reverse in (
                ((qf_ref, lff_ref, kf_ref, vf_ref, of_ref, stf_ref), j, False),
                ((qb_ref, lfb_ref, kb_ref, vb_ref, ob_ref, stb_ref), nblk - 1 - j, True)):
            rows = slice(jj * HG_BLOCK, (jj + 1) * HG_BLOCK)
            gen = _hgrn_block(q_ref[rows, :], lf_ref[rows, :], k_ref[rows, :], v_ref[rows, :], reverse, direct)
            blocks.append((gen, o_ref, st_ref, rows))

    for _stage in range(3):
        for block in blocks:
            next(block[0])
    spans = []
    for gen, o_ref, st_ref, rows in blocks:
        o_ref[rows, :], states, span = gen.send([st_ref[p] for p in range(HG_PAIRS)])
        for p in range(HG_PAIRS):
            st_ref[p] = states[p]
        spans.append(span)
    return functools.reduce(jnp.maximum, spans)


def _hgrn_kernel(*refs):
    *ins, of_ref, ob_ref, span_ref, stf_ref, stb_ref = refs
    span = _hgrn_scan(ins, of_ref, ob_ref, stf_ref, stb_ref, None)
    span_ref[...] = jnp.broadcast_to(span, span_ref.shape)


def _hgrn_safe_kernel(*refs):
    ones_ref, *ins, of_ref, ob_ref, stf_ref, stb_ref, b_ref, k_ref, v_ref = refs
    _hgrn_scan(ins, of_ref, ob_ref, stf_ref, stb_ref, (ones_ref, (b_ref, k_ref, v_ref)))


def _hgrn(hq, lff, kf, lfb, kb, hv, ones_blockdiag, batch, seq):
    tc = HG_ROWS
    nt = seq // tc
    fwd = pl.BlockSpec((tc, HG_WIDTH), lambda b, n: (b * nt + n, 0))
    bwd = pl.BlockSpec((tc, HG_WIDTH), lambda b, n: (b * nt + nt - 1 - n, 0))
    out = jax.ShapeDtypeStruct((batch * seq, HG_WIDTH), F32)
    state = pltpu.VMEM((HG_PAIRS, HG_BLOCK, HG_BLOCK), F32)
    params = pltpu.CompilerParams(dimension_semantics=("parallel", "arbitrary"),
                                  vmem_limit_bytes=VMEM_LIMIT_BYTES)
    operands = (hq, lff, kf, hv, hq, lfb, kb, hv)
    o_f, o_b, span = pl.pallas_call(
        _hgrn_kernel,
        grid=(batch, nt),
        in_specs=[fwd, fwd, fwd, fwd, bwd, bwd, bwd, bwd],
        out_specs=[fwd, bwd, pl.BlockSpec((None, 8, LANES), lambda b, n: (b * nt + n, 0, 0))],
        out_shape=[out, out, jax.ShapeDtypeStruct((batch * nt, 8, LANES), F32)],
        scratch_shapes=[state, state],
        compiler_params=params,
        name="hgrn",
    )(*operands)

    def safe():
        rows = pltpu.VMEM((HG_BLOCK, HG_WIDTH), F32)
        return tuple(pl.pallas_call(
            _hgrn_safe_kernel,
            grid=(batch, nt),
            in_specs=[pl.BlockSpec(ones_blockdiag.shape, lambda b, n: (0, 0)),
                      fwd, fwd, fwd, fwd, bwd, bwd, bwd, bwd],
            out_specs=[fwd, bwd],
            out_shape=[out, out],
            scratch_shapes=[state, state, rows, rows, rows],
            compiler_params=params,
            name="hgrn_safe",
        )(ones_blockdiag, *operands))

    return lax.cond(jnp.max(span) < HG_MAX_SPAN, lambda: (o_f, o_b), safe)


def _route(scores, biased):
    tm = scores.shape[1]
    neg_inf = -jnp.inf
    group_scores = []
    for g in range(N_GROUPS):
        blk = biased[g * EXPERTS_PER_GROUP:(g + 1) * EXPERTS_PER_GROUP, :]
        m1 = jnp.max(blk, axis=0, keepdims=True)
        n_max = jnp.sum(jnp.where(blk == m1, 1.0, 0.0), axis=0, keepdims=True)
        below = jnp.max(jnp.where(blk < m1, blk, neg_inf), axis=0, keepdims=True)
        group_scores.append(m1 + jnp.where(n_max >= 2.0, m1, below))
    gs = jnp.concatenate(group_scores, axis=0)

    gidx = lax.broadcasted_iota(jnp.int32, (N_GROUPS, tm), 0)
    rank = jnp.zeros((N_GROUPS, tm), F32)
    for g in range(N_GROUPS):
        other = gs[g:g + 1, :]
        ahead = jnp.where(other > gs, 1.0, jnp.where(other == gs, jnp.where(gidx > g, 1.0, 0.0), 0.0))
        rank = rank + ahead
    group_on = jnp.where(rank < float(TOPK_GROUPS), 1.0, 0.0)
    expert_on = jnp.concatenate(
        [jnp.broadcast_to(group_on[g:g + 1, :], (EXPERTS_PER_GROUP, tm)) for g in range(N_GROUPS)], axis=0)

    eidx = lax.broadcasted_iota(jnp.int32, (N_EXPERTS, tm), 0).astype(F32)
    work = jnp.where(expert_on > 0.0, biased, neg_inf)
    chosen = jnp.zeros((N_EXPERTS, tm), F32)
    for _ in range(TOP_K):
        best = jnp.max(work, axis=0, keepdims=True)
        first = jnp.min(jnp.where(work == best, eidx, float(N_EXPERTS)), axis=0, keepdims=True)
        pick = eidx == first
        chosen = jnp.where(pick, 1.0, chosen)
        work = jnp.where(pick, neg_inf, work)
    w = jnp.where(chosen > 0.0, scores, 0.0)
    return w / jnp.sum(w, axis=0, keepdims=True) * ROUTED_SCALE


def _mix_kernel(x_ref, att_ref, of_ref, ob_ref, g_ref, wout_ref, hnw_ref, lnw_ref, lnb_ref,
                wrhi_ref, wrlo_ref, rb_ref, ones_ref, x1_ref, gate_ref):
    o = of_ref[...] + ob_ref[...]
    o = o * lax.rsqrt(_group_mean_sq(o, ones_ref[...]) + RMS_EPS) * hnw_ref[...] * g_ref[...].astype(F32)
    mixed = jnp.concatenate([att_ref[...].astype(BF16), o.astype(BF16)], axis=1)
    h = DEEPNORM_ALPHA * x_ref[...] + _dot(mixed, wout_ref[...])
    x1 = _layer_norm(h, lnw_ref[...], lnb_ref[...])
    x1_ref[...] = x1

    x_hi, x_lo = _split_bf16(x1)
    w_hi, w_lo = wrhi_ref[...], wrlo_ref[...]
    logits = _dot_nt(w_hi, x_hi) + _dot_nt(w_hi, x_lo) + _dot_nt(w_lo, x_hi)
    scores = _sigmoid(logits)
    gate = _route(scores, scores + rb_ref[...])
    gate_ref[...] = jnp.concatenate([gate, jnp.zeros_like(gate)], axis=0).T


def _mix(x2, att, o_f, o_b, hg, w_out, hg_norm_w, ln_w, ln_b, w_router, router_bias, ones_blockdiag):
    t = x2.shape[0]
    tm = MIX_ROWS
    wr_hi, wr_lo = _split_bf16(w_router.T)
    args = (x2, att, o_f, o_b, hg, w_out, hg_norm_w[None, :], ln_w[None, :], ln_b[None, :],
            wr_hi, wr_lo, router_bias[:, None], ones_blockdiag)
    row_block = lambda w: pl.BlockSpec((tm, w), lambda i: (i, 0))
    full = lambda a: pl.BlockSpec(a.shape, lambda i: (0,) * a.ndim)
    return pl.pallas_call(
        _mix_kernel,
        grid=(t // tm,),
        in_specs=[row_block(D_MODEL), row_block(ATT_WIDTH), row_block(HG_WIDTH), row_block(HG_WIDTH),
                  row_block(HG_WIDTH)] + [full(a) for a in args[5:]],
        out_specs=[row_block(D_MODEL), row_block(LANES)],
        out_shape=[jax.ShapeDtypeStruct((t, D_MODEL), F32), jax.ShapeDtypeStruct((t, LANES), F32)],
        compiler_params=pltpu.CompilerParams(dimension_semantics=("parallel",),
                                             vmem_limit_bytes=VMEM_LIMIT_BYTES),
        name="mix",
    )(*args)


def _swiglu(xb, w_gate, w_up):
    hg = _dot(xb, w_gate)
    return hg * _sigmoid(hg) * _dot(xb, w_up)


def _moe_kernel(x1_ref, gate_ref, wg_ref, wu_ref, wd_ref, winv_ref, wsg_ref, wsu_ref, wsd_ref, lnw_ref, lnb_ref,
                o_ref, x8_ref, xinv_ref, acc_ref):
    step = pl.program_id(1)

    @pl.when(step == 0)
    def _init():
        x1 = x1_ref[...]
        peak = jnp.maximum(jnp.max(jnp.abs(x1), axis=1, keepdims=True), FP8_TINY)
        x8_ref[...] = (x1 * (FP8_PEAK / peak)).astype(FP8)
        xinv_ref[...] = peak * (1.0 / FP8_PEAK)
        xb = x1.astype(BF16)
        acc_ref[...] = _dot(_swiglu(xb, wsg_ref[...], wsu_ref[...]).astype(BF16), wsd_ref[...])

    x8 = x8_ref[...]
    xinv = xinv_ref[...]
    lane = lax.broadcasted_iota(jnp.int32, gate_ref.shape, 1)
    hidden = []
    for j in range(MOE_GROUP):
        raw_gate = _dot(x8, wg_ref[j])
        g = jnp.sum(jnp.where(lane == step * MOE_GROUP + j, gate_ref[...], 0.0), axis=1, keepdims=True)
        gate_scale = xinv * winv_ref[j, 0:1, 0:1]
        sig = 1.0 / (1.0 + jnp.exp2(raw_gate * (gate_scale * -LOG2_E)))
        row_scale = gate_scale * g * xinv * (winv_ref[j, 0:1, 1:2] * winv_ref[j, 0:1, 2:3])
        hidden.append(raw_gate * sig * _dot(x8, wu_ref[j]) * row_scale)
    hidden = jnp.concatenate(hidden, axis=1)
    peak = jnp.maximum(jnp.max(jnp.abs(hidden), axis=1, keepdims=True), FP8_TINY)
    h8 = (hidden * (FP8_PEAK / peak)).astype(FP8)
    w_down = wd_ref[...].reshape(MOE_GROUP * EXPERT_HIDDEN, D_MODEL)
    acc_ref[...] += _dot(h8, w_down) * (peak * (1.0 / FP8_PEAK))

    @pl.when(step == pl.num_programs(1) - 1)
    def _finish():
        h = DEEPNORM_ALPHA * x1_ref[...] + acc_ref[...]
        o_ref[...] = _layer_norm(h, lnw_ref[...], lnb_ref[...])


def _quantize_kernel(wg_ref, wu_ref, wd_ref, wg8_ref, wu8_ref, wd8_ref, inv_ref):
    lane = lax.broadcasted_iota(jnp.int32, inv_ref.shape, 1)
    inv = jnp.zeros(inv_ref.shape, F32)
    for i, (w_ref, w8_ref) in enumerate(((wg_ref, wg8_ref), (wu_ref, wu8_ref), (wd_ref, wd8_ref))):
        w = w_ref[...]
        peak = jnp.max(jnp.max(jnp.abs(w), axis=0, keepdims=True), axis=1, keepdims=True)
        peak = jnp.maximum(peak, FP8_TINY)
        w8_ref[...] = (w * (FP8_PEAK / peak)).astype(FP8)
        inv = jnp.where(lane == i, peak * (1.0 / FP8_PEAK), inv)
    inv_ref[...] = inv


def _quantize_experts(w_gate, w_up, w_down):
    spec = lambda a: pl.BlockSpec((None,) + a.shape[1:], lambda e: (e, 0, 0))
    fp8 = lambda a: jax.ShapeDtypeStruct(a.shape, FP8)
    inv = jax.ShapeDtypeStruct((N_EXPERTS, 8, LANES), F32)
    return pl.pallas_call(
        _quantize_kernel,
        grid=(N_EXPERTS,),
        in_specs=[spec(w_gate), spec(w_up), spec(w_down)],
        out_specs=[spec(w_gate), spec(w_up), spec(w_down), spec(inv)],
        out_shape=[fp8(w_gate), fp8(w_up), fp8(w_down), inv],
        compiler_params=pltpu.CompilerParams(dimension_semantics=("parallel",),
                                             vmem_limit_bytes=VMEM_LIMIT_BYTES),
        name="quantize",
    )(w_gate, w_up, w_down)


def _moe(x1, gate, w_eg8, w_eu8, w_ed, w_inv, w_sg, w_su, w_sd, ln_w, ln_b):
    t = x1.shape[0]
    tm = MOE_ROWS
    args = (x1, gate, w_eg8, w_eu8, w_ed, w_inv, w_sg, w_su, w_sd, ln_w[None, :], ln_b[None, :])
    row_block = lambda w: pl.BlockSpec((tm, w), lambda i, g: (i, 0))
    full = lambda a: pl.BlockSpec(a.shape, lambda i, g: (0,) * a.ndim)
    experts = lambda a: pl.BlockSpec((MOE_GROUP,) + a.shape[1:], lambda i, g: (g, 0, 0))
    return pl.pallas_call(
        _moe_kernel,
        grid=(t // tm, N_EXPERTS // MOE_GROUP),
        in_specs=[row_block(D_MODEL), row_block(LANES), experts(w_eg8), experts(w_eu8), experts(w_ed),
                  experts(w_inv)] + [full(a) for a in args[6:]],
        out_specs=row_block(D_MODEL),
        out_shape=jax.ShapeDtypeStruct((t, D_MODEL), F32),
        scratch_shapes=[pltpu.VMEM((tm, D_MODEL), FP8), pltpu.VMEM((tm, 1), F32),
                        pltpu.VMEM((tm, D_MODEL), F32)],
        compiler_params=pltpu.CompilerParams(dimension_semantics=("parallel", "arbitrary"),
                                             vmem_limit_bytes=VMEM_LIMIT_BYTES),
        name="moe",
    )(*args)


def _trunk(x, p):
    batch, seq, _ = x.shape
    for rows in (PROJ_ROWS, ATT_Q_ROWS, ATT_K_ROWS, ATT_SAFE_ROWS, HG_ROWS, GRID_W):
        assert seq % rows == 0, (seq, rows)
    for rows in (MIX_ROWS, MOE_ROWS):
        assert (batch * seq) % rows == 0, (batch, seq, rows)
    x2 = x.reshape(batch * seq, D_MODEL)
    qt, k, vt, hq, lff, kf, lfb, kb, hv, hg = _project(
        x2, batch, seq, p["w_in"], p["q_norm_w"], p["k_norm_w"], p["lb"], p["ones"])
    att = _attention(qt, k, vt, p["attn_out_norm_w"], p["k_norm_w"], batch, seq)
    o_f, o_b = _hgrn(hq, lff, kf, lfb, kb, hv, p["ones"], batch, seq)
    x1, gate = _mix(x2, att, o_f, o_b, hg, p["w_out"], p["hgrn_out_norm_w"], p["ln1_w"], p["ln1_b"],
                    p["w_router"], p["router_bias"], p["ones"])
    y = _moe(x1, gate, p["w_exp_gate"], p["w_exp_up"], p["w_exp_down"], p["w_exp_inv"], p["w_sh_gate"],
             p["w_sh_up"], p["w_sh_down"], p["ln2_w"], p["ln2_b"])
    return y.reshape(batch, seq, D_MODEL)


def _prepare(w_in, q_norm_w, k_norm_w, attn_out_norm_w, hgrn_lb_logits, hgrn_out_norm_w, w_out, ln1_w, ln1_b, w_router, router_bias, w_exp_gate, w_exp_up, w_exp_down, w_sh_gate, w_sh_up, w_sh_down, ln2_w, ln2_b):
    layer = 0
    lb_all = jnp.cumsum(jax.nn.softmax(hgrn_lb_logits.astype(F32), axis=0), axis=0)
    head_of = jnp.arange(ATT_WIDTH) // HEAD_DIM
    w_eg8, w_eu8, w_ed8, w_inv = _quantize_experts(w_exp_gate[layer], w_exp_up[layer], w_exp_down[layer])
    return {
        "w_in": w_in[layer].astype(BF16),
        "q_norm_w": q_norm_w[layer], "k_norm_w": k_norm_w[layer],
        "attn_out_norm_w": attn_out_norm_w[layer],
        "lb": lb_all[layer],
        "hgrn_out_norm_w": hgrn_out_norm_w[layer],
        "w_out": w_out[layer].astype(BF16),
        "ln1_w": ln1_w[layer], "ln1_b": ln1_b[layer],
        "w_router": w_router[layer], "router_bias": router_bias[layer],
        "w_exp_gate": w_eg8, "w_exp_up": w_eu8, "w_exp_down": w_ed8, "w_exp_inv": w_inv,
        "w_sh_gate": w_sh_gate[layer].astype(BF16), "w_sh_up": w_sh_up[layer].astype(BF16),
        "w_sh_down": w_sh_down[layer].astype(BF16),
        "ln2_w": ln2_w[layer], "ln2_b": ln2_b[layer],
        "ones": (head_of[:, None] == head_of[None, :]).astype(BF16),
    }


def kernel(x_prompt, x_sample, w_in, q_norm_w, k_norm_w, attn_out_norm_w, hgrn_lb_logits, hgrn_out_norm_w, w_out, ln1_w, ln1_b, w_router, router_bias, w_exp_gate, w_exp_up, w_exp_down, w_sh_gate, w_sh_up, w_sh_down, ln2_w, ln2_b):
    p = _prepare(w_in, q_norm_w, k_norm_w, attn_out_norm_w, hgrn_lb_logits, hgrn_out_norm_w, w_out, ln1_w, ln1_b, w_router, router_bias, w_exp_gate, w_exp_up, w_exp_down, w_sh_gate, w_sh_up, w_sh_down, ln2_w, ln2_b)
    return _trunk(x_prompt, p), _trunk(x_sample, p)
```
